```python
import math
import jax, jax.numpy as jnp
from jax import lax
import numpy as np

D_MODEL = 1024
BATCH = 16
SEQ = 2048
DEPTH = 2

GRID_W = 64
CTX_LEN = 256
N_EVEN = (DEPTH + 1) // 2
N_ODD = DEPTH // 2
RNN_WIDTH = D_MODEL // 2
LRU_HEADS = 8
LRU_HEAD_DIM = RNN_WIDTH // LRU_HEADS
LRU_CONV = 4
LRU_C = 8.0
LRU_A_MIN = 0.9
LRU_A_MAX = 0.999
HYENA_WIDTH = D_MODEL // 2
HYENA_ORDER = 2
HYENA_SHORT_CONV = 3
FILTER_EMB = 33
FILTER_HIDDEN = 64
HYENA_DECAY_TARGET = 1e-2
HYENA_FAST_PCT = 0.3
HYENA_SLOW_PCT = 1.5
HYENA_SHIFT = 0.05
IN_WIDTH = 2 * RNN_WIDTH + (HYENA_ORDER + 1) * HYENA_WIDTH
CONF_KERNEL = 31
N_EXPERTS = 16
EC_CAPACITY = 2
EXPERT_FF = 1024
N_MOD = 6
EPS = 1e-6

kernel_name = "hybrid_rglru_hyena_conformer_ecmoe_dit"


def _rmsnorm(x, g):
    xf = x.astype(jnp.float32)
    y = xf * lax.rsqrt(jnp.mean(xf * xf, axis=-1, keepdims=True) + EPS)
    return (y * g.astype(jnp.float32)).astype(x.dtype)


def _layernorm(x, g, b):
    xf = x.astype(jnp.float32)
    mu = jnp.mean(xf, axis=-1, keepdims=True)
    var = jnp.mean(jnp.square(xf - mu), axis=-1, keepdims=True)
    y = (xf - mu) * lax.rsqrt(var + EPS) * g.astype(jnp.float32) + b.astype(jnp.float32)
    return y.astype(x.dtype)


def _adaln(cvec, w_mod, b_mod, n_chunks):
    m = jax.nn.silu(cvec) @ w_mod[:, :n_chunks * D_MODEL] + b_mod[:n_chunks * D_MODEL]
    return [m[:, None, j * D_MODEL:(j + 1) * D_MODEL] for j in range(n_chunks)]


def _modulate(h, shift, scale):
    return h * (1 + scale) + shift


def _dwconv1d(x, w, b):
    k = w.shape[0]
    y = lax.conv_general_dilated(x, w[:, None, :], window_strides=(1,),
                                 padding=[((k - 1) // 2, k // 2)],
                                 dimension_numbers=('NWC', 'WIO', 'NWC'),
                                 feature_group_count=x.shape[-1])
    return y + b


def _dwconv_vertical(x, w, b, rows):
    bsz, n, ch = x.shape
    k = w.shape[0]
    grid = x.reshape(bsz, rows, GRID_W, ch)
    y = lax.conv_general_dilated(grid, w[:, None, None, :], window_strides=(1, 1),
                                 padding=[((k - 1) // 2, k // 2), (0, 0)],
                                 dimension_numbers=('NHWC', 'HWIO', 'NHWC'),
                                 feature_group_count=ch)
    return y.reshape(bsz, n, ch) + b


def _lin_combine(left, right):
    a_l, b_l = left
    a_r, b_r = right
    return a_l * a_r, a_r * b_l + b_r


def _rglru(xa_raw, p, h0):
    f32 = jnp.float32
    xa = _dwconv1d(xa_raw, p['conv_a_w'], p['conv_a_b'])
    bsz, n, _ = xa.shape
    xh = xa.reshape(bsz, n, LRU_HEADS, LRU_HEAD_DIM)

    def gate(w, b):
        z = jnp.einsum('bnhi,khij->kbnhj', xh, w).reshape(2, bsz, n, RNN_WIDTH)
        return jax.nn.sigmoid((z + b[:, None, None, :]).astype(f32))

    r = gate(p['lru_wr'], p['lru_br'])
    i = gate(p['lru_wi'], p['lru_bi'])
    log_a = -LRU_C * r * jax.nn.softplus(-p['lru_lam'].astype(f32))[:, None, None, :]
    a = jnp.exp(log_a)
    u = jnp.sqrt(-jnp.expm1(2.0 * log_a)) * i * xa.astype(f32)[None]
    a = jnp.stack([a[0], jnp.flip(a[1], axis=1)])
    u = jnp.stack([u[0], jnp.flip(u[1], axis=1)])
    a_cum, h = lax.associative_scan(_lin_combine, (a, u), axis=2)
    h = h + a_cum * h0[:, :, None, :]
    return h, h[:, :, -1]


def _hyena_filters_fft(n, p):
    f32 = jnp.float32
    t = jnp.linspace(0.0, 1.0, n, dtype=f32)[:, None]
    bands = (FILTER_EMB - 1) // 2
    w = (2.0 * math.pi / n) * jnp.arange(n, dtype=f32)[:, None]
    f = jnp.linspace(1e-4, bands - 1, bands, dtype=f32)[None, :]
    z = jnp.concatenate([t, jnp.cos(f * w), -jnp.sin(f * w)], axis=-1)
    freq = p['filt_freq'].astype(f32)
    hid = jnp.sin(freq[0] * (z @ p['filt_w1'].astype(f32) + p['filt_b1'].astype(f32)))
    hid = jnp.sin(freq[1] * (hid @ p['filt_w2'].astype(f32) + p['filt_b2'].astype(f32)))
    filt = (hid @ p['filt_w3'].astype(f32) + p['filt_b3'].astype(f32))
    filt = filt.reshape(n, HYENA_ORDER, 2, HYENA_WIDTH)
    deltas = jnp.abs(jnp.linspace(math.log(HYENA_DECAY_TARGET) / HYENA_SLOW_PCT,
                                  math.log(HYENA_DECAY_TARGET) / HYENA_FAST_PCT,
                                  HYENA_WIDTH, dtype=f32))
    window = jnp.exp(-t * deltas[None, :]) + HYENA_SHIFT
    filt = filt * window[:, None, None, :]
    fwd, bwd = filt[:, :, 0], filt[:, :, 1]
    kern = jnp.concatenate([fwd, jnp.zeros((1, HYENA_ORDER, HYENA_WIDTH), f32), bwd[:0:-1]], axis=0)
    return jnp.fft.rfft(kern, axis=0)


def _fftconv(u, kf, bias):
    n = u.shape[1]
    uf = u.astype(jnp.float32)
    y = jnp.fft.irfft(jnp.fft.rfft(uf, n=2 * n, axis=1) * kf[None], n=2 * n, axis=1)[:, :n]
    return (y + uf * bias.astype(jnp.float32)).astype(u.dtype)


def _even_mixer(h, p, h0):
    proj = h @ p['w_in']
    gate = proj[..., :RNN_WIDTH]
    xa_raw = proj[..., RNN_WIDTH:2 * RNN_WIDTH]
    hy = proj[..., 2 * RNN_WIDTH:]
    hs, h_last = _rglru(xa_raw, p, h0)
    y_a = jax.nn.gelu(gate) * (hs[0] + jnp.flip(hs[1], axis=1)).astype(h.dtype)
    u = _dwconv1d(hy, p['conv_b_w'], p['conv_b_b'])
    v, *gates = jnp.split(u, HYENA_ORDER + 1, axis=-1)
    kf = _hyena_filters_fft(h.shape[1], p)
    z = v
    for o in range(HYENA_ORDER):
        z = gates[o] * _fftconv(z, kf[:, o], p['filt_bias'][o])
    return jnp.concatenate([y_a, z], axis=-1) @ p['w_out'], h_last


def _conformer_conv(h, p, rows):
    u = h @ p['cf_w1'] + p['cf_b1']
    u = u[..., :D_MODEL] * jax.nn.sigmoid(u[..., D_MODEL:])
    if rows is None:
        u = _dwconv1d(u, p['cf_dw_w'], p['cf_dw_b'])
    else:
        u = _dwconv_vertical(u, p['cf_dw_w'], p['cf_dw_b'], rows)
    u = jax.nn.silu(_layernorm(u, p['cf_ln_g'], p['cf_ln_b']))
    return u @ p['cf_w2'] + p['cf_b2']


def _ec_moe(h, router, w1, w3, w2):
    bsz, n, _ = h.shape
    cap = EC_CAPACITY * n // N_EXPERTS
    probs = jax.nn.softmax(jnp.einsum('bnd,de->ben', h, router).astype(jnp.float32), axis=1)
    gates, idx = lax.top_k(probs, cap)
    bidx = jnp.arange(bsz)[:, None, None]
    xs = h[bidx, idx]
    a = jnp.einsum('becd,edf->becf', xs, w1)
    g = jnp.einsum('becd,edf->becf', xs, w3)
    y = jnp.einsum('becf,efd->becd', jax.nn.silu(a) * g, w2) * gates[..., None].astype(h.dtype)
    return jnp.zeros_like(h).at[bidx, idx].add(y)


def setup_inputs(seed: int = 0) -> dict:
    key = jax.random.key(seed)
    ks = iter(jax.random.split(key, 48))
    f32 = jnp.float32
    D = D_MODEL

    def nrm(shape, scale):
        return jax.random.normal(next(ks), shape, f32) * scale

    x = nrm((BATCH, SEQ, D), 1.0)
    c = nrm((BATCH, D), 1.0)
    ctx = nrm((BATCH, CTX_LEN, D), 1.0)
    c_ctx = nrm((D,), 1.0)
    norm1_g = 1.0 + nrm((DEPTH, D), 0.1)
    norm2_g = 1.0 + nrm((DEPTH, D), 0.1)
    w_mod = nrm((DEPTH, D, N_MOD * D), 0.5 * D ** -0.5)
    b_mod = nrm((DEPTH, N_MOD * D), 0.01)
    w_in = nrm((N_EVEN, D, IN_WIDTH), D ** -0.5)
    conv_a_w = nrm((N_EVEN, LRU_CONV, RNN_WIDTH), LRU_CONV ** -0.5)
    conv_a_b = nrm((N_EVEN, RNN_WIDTH), 0.01)
    lru_wr = nrm((N_EVEN, 2, LRU_HEADS, LRU_HEAD_DIM, LRU_HEAD_DIM), LRU_HEAD_DIM ** -0.5)
    lru_br = nrm((N_EVEN, 2, RNN_WIDTH), 0.01)
    lru_wi = nrm((N_EVEN, 2, LRU_HEADS, LRU_HEAD_DIM, LRU_HEAD_DIM), LRU_HEAD_DIM ** -0.5)
    lru_bi = nrm((N_EVEN, 2, RNN_WIDTH), 0.01)
    lo, hi = LRU_A_MIN ** (1.0 / LRU_C), LRU_A_MAX ** (1.0 / LRU_C)
    ua = jax.random.uniform(next(ks), (N_EVEN, 2, RNN_WIDTH), f32, lo, hi)
    lru_lam = jnp.log(ua) - jnp.log1p(-ua)
    conv_b_w = nrm((N_EVEN, HYENA_SHORT_CONV, (HYENA_ORDER + 1) * HYENA_WIDTH), HYENA_SHORT_CONV ** -0.5)
    conv_b_b = nrm((N_EVEN, (HYENA_ORDER + 1) * HYENA_WIDTH), 0.01)
    filt_w1 = nrm((N_EVEN, FILTER_EMB, FILTER_HIDDEN), FILTER_EMB ** -0.5)
    filt_b1 = nrm((N_EVEN, FILTER_HIDDEN), 0.1)
    filt_w2 = nrm((N_EVEN, FILTER_HIDDEN, FILTER_HIDDEN), FILTER_HIDDEN ** -0.5)
    filt_b2 = nrm((N_EVEN, FILTER_HIDDEN), 0.1)
    filt_w3 = nrm((N_EVEN, FILTER_HIDDEN, HYENA_ORDER * 2 * HYENA_WIDTH), 0.1 * FILTER_HIDDEN ** -0.5)
    filt_b3 = nrm((N_EVEN, HYENA_ORDER * 2 * HYENA_WIDTH), 0.01)
    filt_freq = 1.0 + nrm((N_EVEN, 2, FILTER_HIDDEN), 0.1)
    filt_bias = nrm((N_EVEN, HYENA_ORDER, HYENA_WIDTH), 0.5)
    w_out = nrm((N_EVEN, RNN_WIDTH + HYENA_WIDTH, D), (RNN_WIDTH + HYENA_WIDTH) ** -0.5)
    cf_w1 = nrm((N_ODD, D, 2 * D), D ** -0.5)
    cf_b1 = nrm((N_ODD, 2 * D), 0.01)
    cf_dw_w = nrm((N_ODD, CONF_KERNEL, D), CONF_KERNEL ** -0.5)
    cf_dw_b = nrm((N_ODD, D), 0.01)
    cf_ln_g = 1.0 + nrm((N_ODD, D), 0.1)
    cf_ln_b = nrm((N_ODD, D), 0.01)
    cf_w2 = nrm((N_ODD, D, D), D ** -0.5)
    cf_b2 = nrm((N_ODD, D), 0.01)
    router = nrm((DEPTH, D, N_EXPERTS), D ** -0.5)
    exp_w1 = nrm((DEPTH, N_EXPERTS, D, EXPERT_FF), D ** -0.5)
    exp_w3 = nrm((DEPTH, N_EXPERTS, D, EXPERT_FF), D ** -0.5)
    exp_w2 = nrm((DEPTH, N_EXPERTS, EXPERT_FF, D), EXPERT_FF ** -0.5)
    final_g = 1.0 + nrm((D,), 0.1)
    return {"x": x, "c": c, "ctx": ctx, "c_ctx": c_ctx,
            "norm1_g": norm1_g, "norm2_g": norm2_g, "w_mod": w_mod, "b_mod": b_mod,
            "w_in": w_in, "conv_a_w": conv_a_w, "conv_a_b": conv_a_b,
            "lru_wr": lru_wr, "lru_br": lru_br, "lru_wi": lru_wi, "lru_bi": lru_bi, "lru_lam": lru_lam,
            "conv_b_w": conv_b_w, "conv_b_b": conv_b_b,
            "filt_w1": filt_w1, "filt_b1": filt_b1, "filt_w2": filt_w2, "filt_b2": filt_b2,
            "filt_w3": filt_w3, "filt_b3": filt_b3, "filt_freq": filt_freq, "filt_bias": filt_bias,
            "w_out": w_out,
            "cf_w1": cf_w1, "cf_b1": cf_b1, "cf_dw_w": cf_dw_w, "cf_dw_b": cf_dw_b,
            "cf_ln_g": cf_ln_g, "cf_ln_b": cf_ln_b, "cf_w2": cf_w2, "cf_b2": cf_b2,
            "router": router, "exp_w1": exp_w1, "exp_w3": exp_w3, "exp_w2": exp_w2,
            "final_g": final_g}


def reference(x, c, ctx, c_ctx, norm1_g, norm2_g, w_mod, b_mod,
              w_in, conv_a_w, conv_a_b, lru_wr, lru_br, lru_wi, lru_bi, lru_lam,
              conv_b_w, conv_b_b, filt_w1, filt_b1, filt_w2, filt_b2, filt_w3, filt_b3,
              filt_freq, filt_bias, w_out,
              cf_w1, cf_b1, cf_dw_w, cf_dw_b, cf_ln_g, cf_ln_b, cf_w2, cf_b2,
              router, exp_w1, exp_w3, exp_w2, final_g):
    n_lat = x.shape[1]
    rows = n_lat // GRID_W
    last_ctx_layer = ((DEPTH - 1) // 2) * 2
    xl, xc = x, ctx
    for l in range(DEPTH):
        carry_ctx = l < last_ctx_layer
        sh1, sc1, g1, sh2, sc2, g2 = _adaln(c, w_mod[l], b_mod[l], N_MOD)
        hl = _modulate(_rmsnorm(xl, norm1_g[l]), sh1, sc1)
        if l % 2 == 0 or carry_ctx:
            cm = _adaln(c_ctx[None], w_mod[l], b_mod[l], N_MOD if carry_ctx else 2)
            hc = _modulate(_rmsnorm(xc, norm1_g[l]), cm[0], cm[1])
        if l % 2 == 0:
            e = l // 2
            pe = {'w_in': w_in[e], 'conv_a_w': conv_a_w[e], 'conv_a_b': conv_a_b[e],
                  'lru_wr': lru_wr[e], 'lru_br': lru_br[e], 'lru_wi': lru_wi[e], 'lru_bi': lru_bi[e],
                  'lru_lam': lru_lam[e], 'conv_b_w': conv_b_w[e], 'conv_b_b': conv_b_b[e],
                  'filt_w1': filt_w1[e], 'filt_b1': filt_b1[e], 'filt_w2': filt_w2[e],
                  'filt_b2': filt_b2[e], 'filt_w3': filt_w3[e], 'filt_b3': filt_b3[e],
                  'filt_freq': filt_freq[e], 'filt_bias': filt_bias[e], 'w_out': w_out[e]}
            h0_ctx = jnp.zeros((2, xc.shape[0], RNN_WIDTH), jnp.float32)
            if carry_ctx:
                yc, ctx_state = _even_mixer(hc, pe, h0_ctx)
            else:
                _, ctx_state = _rglru(hc @ pe['w_in'][:, RNN_WIDTH:2 * RNN_WIDTH], pe, h0_ctx)
            yl, _ = _even_mixer(hl, pe, ctx_state)
        else:
            o = l // 2
            po = {'cf_w1': cf_w1[o], 'cf_b1': cf_b1[o], 'cf_dw_w': cf_dw_w[o], 'cf_dw_b': cf_dw_b[o],
                  'cf_ln_g': cf_ln_g[o], 'cf_ln_b': cf_ln_b[o], 'cf_w2': cf_w2[o], 'cf_b2': cf_b2[o]}
            yl = _conformer_conv(hl, po, rows)
            if carry_ctx:
                yc = _conformer_conv(hc, po, None)
        xl = xl + g1 * yl
        xl = xl + g2 * _ec_moe(_modulate(_rmsnorm(xl, norm2_g[l]), sh2, sc2),
                               router[l], exp_w1[l], exp_w3[l], exp_w2[l])
        if carry_ctx:
            xc = xc + cm[2] * yc
            xc = xc + cm[5] * _ec_moe(_modulate(_rmsnorm(xc, norm2_g[l]), cm[3], cm[4]),
                                      router[l], exp_w1[l], exp_w3[l], exp_w2[l])
    return _rmsnorm(xl, final_g)
```

```python
import functools
import math

import numpy as np
import jax
import jax.numpy as jnp
from jax import lax
from jax.experimental import pallas as pl
from jax.experimental.pallas import tpu as pltpu

F32 = jnp.float32
BF16 = jnp.bfloat16
HIGHEST = lax.Precision.HIGHEST

EPS = 1e-6
GRID_W = 64
N_MOD = 6
LRU_HEADS = 8
LRU_C = 8.0
HYENA_ORDER = 2
HYENA_DECAY_TARGET = 1e-2
HYENA_FAST_PCT = 0.3
HYENA_SLOW_PCT = 1.5
HYENA_SHIFT = 0.05
EC_CAPACITY = 2

V7X_VMEM_BYTES = 64 * 1024 * 1024
VMEM_LIMIT = V7X_VMEM_BYTES - 8 * 1024 * 1024
SUBLANES = 8


def _params(n_axes):
    return pltpu.CompilerParams(dimension_semantics=("arbitrary",) * n_axes,
                                vmem_limit_bytes=VMEM_LIMIT)


def _single(block_shape, index_map):
    return pl.BlockSpec(block_shape, index_map, pipeline_mode=pl.Buffered(1))


def _silu(x):
    return x * jax.nn.sigmoid(x)


def _norm_mod(xf, g, shift, scale):
    ms = jnp.mean(xf * xf, axis=-1, keepdims=True)
    y = xf * lax.rsqrt(ms + EPS) * g
    return y * (1.0 + scale) + shift


def _adaln_kernel(c_ref, w_ref, b_ref, o_ref):
    s = _silu(c_ref[...])
    o_ref[...] = jnp.dot(s, w_ref[...], precision=HIGHEST, preferred_element_type=F32) + b_ref[...]


def _adaln(cvec, w_mod, b_mod, tc=512):
    depth, d, dm = w_mod.shape
    r = cvec.shape[0]
    return pl.pallas_call(
        _adaln_kernel,
        grid=(depth, dm // tc),
        in_specs=[pl.BlockSpec((r, d), lambda l, j: (0, 0)),
                  pl.BlockSpec((None, d, tc), lambda l, j: (l, 0, j)),
                  pl.BlockSpec((None, 1, tc), lambda l, j: (l, 0, j))],
        out_specs=pl.BlockSpec((None, r, tc), lambda l, j: (l, 0, j)),
        out_shape=jax.ShapeDtypeStruct((depth, r, dm), F32),
        compiler_params=_params(2),
        name="adaln",
    )(cvec, w_mod, b_mod.reshape(depth, 1, dm))


def _norm_matmul_kernel(x_ref, g_ref, sh_ref, sc_ref, w_ref, *rest, has_bias, glu):
    o_ref = rest[-1]
    h = _norm_mod(x_ref[...], g_ref[...], sh_ref[...], sc_ref[...])
    y = jnp.dot(h.astype(BF16), w_ref[...], preferred_element_type=F32)
    if has_bias:
        y = y + rest[0][...]
    if glu:
        half = y.shape[-1] // 2
        y = y[:, :half] * jax.nn.sigmoid(y[:, half:])
    o_ref[...] = y.astype(o_ref.dtype)


def _norm_matmul(x, g, mods, mod_row, shift_j, scale_j, w_bf16, bias=None, glu=False, tn=512,
                 out_dtype=F32):
    b, n, d = x.shape
    nout = w_bf16.shape[1]
    tn = min(tn, n)
    out_w = nout // 2 if glu else nout
    in_specs = [pl.BlockSpec((None, tn, d), lambda i, j: (i, j, 0)),
                pl.BlockSpec((1, d), lambda i, j: (0, 0)),
                pl.BlockSpec((None, 1, d), lambda i, j: (mod_row(i) * N_MOD + shift_j, 0, 0)),
                pl.BlockSpec((None, 1, d), lambda i, j: (mod_row(i) * N_MOD + scale_j, 0, 0)),
                _single((d, nout), lambda i, j: (0, 0))]
    args = [x, g.reshape(1, d), mods, mods, w_bf16]
    if bias is not None:
        in_specs.append(pl.BlockSpec((1, nout), lambda i, j: (0, 0)))
        args.append(bias.reshape(1, nout))
    return pl.pallas_call(
        functools.partial(_norm_matmul_kernel, has_bias=bias is not None, glu=glu),
        grid=(b, n // tn),
        in_specs=in_specs,
        out_specs=pl.BlockSpec((None, tn, out_w), lambda i, j: (i, j, 0)),
        out_shape=jax.ShapeDtypeStruct((b, n, out_w), out_dtype),
        compiler_params=_params(2),
        name="norm_matmul",
    )(*args)


def _shift_rows(cur, k, prev8, next8, has_prev, has_next):
    t = cur.shape[0]
    row = lax.broadcasted_iota(jnp.int32, cur.shape, 0)
    if k < 0:
        out = pltpu.roll(cur, -k, 0)
        for r in range(-k):
            src = prev8[SUBLANES + k + r:SUBLANES + k + r + 1, :]
            fill = jnp.where(has_prev, src, 0.0)
            out = jnp.where(row == r, fill, out)
    else:
        out = pltpu.roll(cur, t - k, 0)
        for r in range(k):
            src = next8[r:r + 1, :]
            fill = jnp.where(has_next, src, 0.0)
            out = jnp.where(row == t - k + r, fill, out)
    return out


def _dwconv_chunk(x_ref, s, t, n, w_ref, b_ref, pad_left):
    cur = x_ref[pl.ds(s, t), :]
    prev8 = x_ref[pl.ds(pl.multiple_of(jnp.maximum(s - SUBLANES, 0), SUBLANES), SUBLANES), :]
    next8 = x_ref[pl.ds(pl.multiple_of(jnp.minimum(s + t, n - SUBLANES), SUBLANES), SUBLANES), :]
    has_prev = s > 0
    has_next = s + t < n
    k_taps = w_ref.shape[0]
    acc = None
    for k in range(k_taps):
        off = k - pad_left
        xs = cur if off == 0 else _shift_rows(cur, off, prev8, next8, has_prev, has_next)
        term = xs * w_ref[k:k + 1, :]
        acc = term if acc is None else acc + term
    return acc + b_ref[...]


GATE_CHUNK = 256
SCAN_CHUNK = 64


def _scan_rows(a, b, reverse):
    t = a.shape[0]
    row = lax.broadcasted_iota(jnp.int32, a.shape, 0)
    s = 1
    while s < t:
        if reverse:
            valid = row < t - s
            sh = t - s
        else:
            valid = row >= s
            sh = s
        a_s = jnp.where(valid, pltpu.roll(a, sh, 0), 1.0)
        b_s = jnp.where(valid, pltpu.roll(b, sh, 0), 0.0)
        b = a * b_s + b
        a = a * a_s
        s *= 2
    return a, b


def _gelu_tanh(x):
    return 0.5 * x * (1.0 + jnp.tanh(math.sqrt(2.0 / math.pi) * (x + 0.044715 * (x * x * x))))


def _rglru_kernel(*refs, n, with_gate):
    if with_gate:
        (xa_ref, gate_ref, cw_ref, cb_ref, wg_ref, bg_ref, lam_ref, h0_ref,
         y_ref, hl_ref, af_s, uf_s, ab_s, ub_s) = refs
    else:
        (xa_ref, cw_ref, cb_ref, wg_ref, bg_ref, lam_ref, h0_ref,
         hl_ref, af_s, uf_s, ab_s, ub_s) = refs
    w = xa_ref.shape[-1]
    tg = min(GATE_CHUNK, n)
    ts = min(SCAN_CHUNK, n)
    lam = lam_ref[...]
    sp = jnp.maximum(-lam, 0.0) + jnp.log(1.0 + jnp.exp(-jnp.abs(lam)))

    def gates(ci, carry):
        s = pl.multiple_of(ci * tg, tg)
        xa = _dwconv_chunk(xa_ref, s, tg, n, cw_ref, cb_ref, 1)
        z = jnp.dot(xa.astype(BF16), wg_ref[...], preferred_element_type=F32) + bg_ref[...]
        for d, (a_s, u_s) in enumerate(((af_s, uf_s), (ab_s, ub_s))):
            r = jax.nn.sigmoid(z[:, (2 * d) * w:(2 * d + 1) * w])
            i = jax.nn.sigmoid(z[:, (2 * d + 1) * w:(2 * d + 2) * w])
            log_a = (-LRU_C) * r * sp[d:d + 1, :]
            a_s[pl.ds(s, tg), :] = jnp.exp(log_a)
            u_s[pl.ds(s, tg), :] = jnp.sqrt(1.0 - jnp.exp(2.0 * log_a)) * i * xa
        return carry

    lax.fori_loop(0, n // tg, gates, 0)

    nc = n // ts

    def fwd(ci, carry):
        s = pl.multiple_of(ci * ts, ts)
        a_c, h_l = _scan_rows(af_s[pl.ds(s, ts), :], uf_s[pl.ds(s, ts), :], False)
        h = h_l + a_c * carry
        uf_s[pl.ds(s, ts), :] = h
        return h[ts - 1:ts, :]

    hf_last = lax.fori_loop(0, nc, fwd, h0_ref[0:1, :])

    def bwd(ci, carry):
        s = pl.multiple_of((nc - 1 - ci) * ts, ts)
        a_c, h_l = _scan_rows(ab_s[pl.ds(s, ts), :], ub_s[pl.ds(s, ts), :], True)
        h = h_l + a_c * carry
        if with_gate:
            hsum = uf_s[pl.ds(s, ts), :] + h
            y_ref[pl.ds(s, ts), :] = (_gelu_tanh(gate_ref[pl.ds(s, ts), :]) * hsum).astype(y_ref.dtype)
        return h[0:1, :]

    hb_last = lax.fori_loop(0, nc, bwd, h0_ref[1:2, :])
    hl_ref[0:1, :] = hf_last
    hl_ref[1:2, :] = hb_last


def _rglru(xa_src, xa_col, gate_src, gate_col, conv_w, conv_b, wg_bf16, bg, lam, h0, out_dtype=F32):
    b, n, _ = xa_src.shape
    w = conv_w.shape[1]
    with_gate = gate_src is not None
    seq = lambda col: pl.BlockSpec((None, n, w), lambda i: (i, 0, col))
    full = lambda shape: pl.BlockSpec(shape, lambda i: tuple(0 for _ in shape))
    in_specs = [seq(xa_col)]
    args = [xa_src]
    if with_gate:
        in_specs.append(seq(gate_col))
        args.append(gate_src)
    in_specs += [full(conv_w.shape), full((1, w)), _single(wg_bf16.shape, lambda i: (0, 0)),
                 full((1, 4 * w)), full((2, w)), pl.BlockSpec((None, 2, w), lambda i: (i, 0, 0))]
    args += [conv_w, conv_b.reshape(1, w), wg_bf16, bg.reshape(1, 4 * w), lam, h0]
    out_specs = [pl.BlockSpec((None, 2, w), lambda i: (i, 0, 0))]
    out_shape = [jax.ShapeDtypeStruct((b, 2, w), F32)]
    if with_gate:
        out_specs.insert(0, pl.BlockSpec((None, n, w), lambda i: (i, 0, 0)))
        out_shape.insert(0, jax.ShapeDtypeStruct((b, n, w), out_dtype))
    outs = pl.pallas_call(
        functools.partial(_rglru_kernel, n=n, with_gate=with_gate),
        grid=(b,),
        in_specs=in_specs,
        out_specs=out_specs,
        out_shape=out_shape,
        scratch_shapes=[pltpu.VMEM((n, w), F32)] * 4,
        compiler_params=_params(1),
        name="rglru_gated" if with_gate else "rglru_state",
    )(*args)
    return (outs[0], outs[1]) if with_gate else (None, outs[0])


def _lru_gate_weights(lru_wr, lru_wi, lru_br, lru_bi):
    def dense(wh):
        heads, hd, _ = wh.shape
        eye = jnp.eye(heads, dtype=wh.dtype)
        return jnp.einsum('hij,hg->higj', wh, eye).reshape(heads * hd, heads * hd)
    wg = jnp.concatenate([dense(lru_wr[0]), dense(lru_wi[0]), dense(lru_wr[1]), dense(lru_wi[1])], axis=1)
    bg = jnp.concatenate([lru_br[0], lru_bi[0], lru_br[1], lru_bi[1]], axis=0)
    return wg.astype(BF16), bg


@functools.lru_cache(maxsize=None)
def _dft_tables(n):
    m = 2 * n
    k = (np.arange(n, dtype=np.int64)[:, None] * np.arange(n, dtype=np.int64)[None, :]) % m
    ang = 2.0 * np.pi * np.arange(m, dtype=np.float64) / m
    cm = np.cos(ang)[k].astype(np.float32)
    sm = np.sin(ang)[k].astype(np.float32)
    return cm, sm


@functools.lru_cache(maxsize=None)
def _filter_features(n, emb):
    t = np.linspace(0.0, 1.0, n, dtype=np.float32)[:, None]
    bands = (emb - 1) // 2
    w = (np.float32(2.0 * math.pi / n) * np.arange(n, dtype=np.float32))[:, None]
    f = np.linspace(1e-4, bands - 1, bands, dtype=np.float32)[None, :]
    z = np.concatenate([t, np.cos(f * w), -np.sin(f * w)], axis=-1).astype(np.float32)
    return z


@functools.lru_cache(maxsize=None)
def _hyena_deltas(width):
    d = np.abs(np.linspace(math.log(HYENA_DECAY_TARGET) / HYENA_SLOW_PCT,
                           math.log(HYENA_DECAY_TARGET) / HYENA_FAST_PCT, width, dtype=np.float32))
    return d.reshape(1, width).astype(np.float32)


def _filter_kernel(z_ref, t_ref, w1_ref, b1_ref, w2_ref, b2_ref, w3_ref, b3_ref, fr_ref, dl_ref,
                   fsum_ref, fdif_ref, kn_ref, *, n, tn):
    j = pl.program_id(0)
    hw = dl_ref.shape[-1]
    hid = jnp.sin(fr_ref[0:1, :] * (jnp.dot(z_ref[...], w1_ref[...], precision=HIGHEST,
                                            preferred_element_type=F32) + b1_ref[...]))
    hid = jnp.sin(fr_ref[1:2, :] * (jnp.dot(hid, w2_ref[...], precision=HIGHEST,
                                            preferred_element_type=F32) + b2_ref[...]))
    filt = jnp.dot(hid, w3_ref[...], precision=HIGHEST, preferred_element_type=F32) + b3_ref[...]
    window = jnp.exp(-t_ref[...] * dl_ref[...]) + HYENA_SHIFT
    row = lax.broadcasted_iota(jnp.int32, (tn, hw), 0) + j * tn
    alt = jnp.where((row & 1) == 0, 1.0, -1.0)

    @pl.when(j == 0)
    def _():
        kn_ref[...] = jnp.zeros_like(kn_ref)

    for o in range(HYENA_ORDER):
        fwd = filt[:, (2 * o) * hw:(2 * o + 1) * hw] * window
        bwd = filt[:, (2 * o + 1) * hw:(2 * o + 2) * hw] * window
        bwd = jnp.where(row == 0, 0.0, bwd)
        fs = fwd + bwd
        fsum_ref[o] = fs
        fdif_ref[o] = bwd - fwd
        kn_ref[o] += jnp.sum(fs * alt, axis=0, keepdims=True) * (1.0 / (2 * n))


def _hyena_filters(n, filt_w1, filt_b1, filt_w2, filt_b2, filt_w3, filt_b3, filt_freq, tn=256):
    emb, hidden = filt_w1.shape
    hw = filt_w3.shape[1] // (2 * HYENA_ORDER)
    tn = min(tn, n)
    z = jnp.asarray(_filter_features(n, emb))
    t = jnp.asarray(np.linspace(0.0, 1.0, n, dtype=np.float32)[:, None])
    dl = jnp.asarray(_hyena_deltas(hw))
    full = lambda shape: pl.BlockSpec(shape, lambda j: tuple(0 for _ in shape))
    return pl.pallas_call(
        functools.partial(_filter_kernel, n=n, tn=tn),
        grid=(n // tn,),
        in_specs=[pl.BlockSpec((tn, emb), lambda j: (j, 0)),
                  pl.BlockSpec((tn, 1), lambda j: (j, 0)),
                  full((emb, hidden)), full((1, hidden)), full((hidden, hidden)), full((1, hidden)),
                  full((hidden, 2 * HYENA_ORDER * hw)), full((1, 2 * HYENA_ORDER * hw)),
                  full((2, hidden)), full((1, hw))],
        out_specs=[pl.BlockSpec((HYENA_ORDER, tn, hw), lambda j: (0, j, 0)),
                   pl.BlockSpec((HYENA_ORDER, tn, hw), lambda j: (0, j, 0)),
                   pl.BlockSpec((HYENA_ORDER, 1, hw), lambda j: (0, 0, 0))],
        out_shape=[jax.ShapeDtypeStruct((HYENA_ORDER, n, hw), F32),
                   jax.ShapeDtypeStruct((HYENA_ORDER, n, hw), F32),
                   jax.ShapeDtypeStruct((HYENA_ORDER, 1, hw), F32)],
        compiler_params=_params(1),
        name="hyena_filter",
    )(z, t, filt_w1, filt_b1.reshape(1, -1), filt_w2, filt_b2.reshape(1, -1),
      filt_w3, filt_b3.reshape(1, -1), filt_freq, dl)


def _spectrum_kernel(cm_ref, sm_ref, fsum_ref, fdif_ref, kr_ref, ki_ref, *, n, tm):
    j = pl.program_id(1)
    row = lax.broadcasted_iota(jnp.int32, kr_ref.shape, 0) + j * tm
    scale = jnp.where(row == 0, 1.0 / (2 * n), 2.0 / (2 * n))
    kr_ref[...] = scale * jnp.dot(cm_ref[...], fsum_ref[...], precision=HIGHEST,
                                  preferred_element_type=F32)
    ki_ref[...] = scale * jnp.dot(sm_ref[...], fdif_ref[...], precision=HIGHEST,
                                  preferred_element_type=F32)


def _hyena_spectrum(cm, sm, fsum, fdif, tm=256):
    order, n, hw = fsum.shape
    tm = min(tm, n)
    return pl.pallas_call(
        functools.partial(_spectrum_kernel, n=n, tm=tm),
        grid=(order, n // tm),
        in_specs=[pl.BlockSpec((tm, n), lambda o, j: (j, 0)),
                  pl.BlockSpec((tm, n), lambda o, j: (j, 0)),
                  pl.BlockSpec((None, n, hw), lambda o, j: (o, 0, 0)),
                  pl.BlockSpec((None, n, hw), lambda o, j: (o, 0, 0))],
        out_specs=[pl.BlockSpec((None, tm, hw), lambda o, j: (o, j, 0)),
                   pl.BlockSpec((None, tm, hw), lambda o, j: (o, j, 0))],
        out_shape=[jax.ShapeDtypeStruct((order, n, hw), F32)] * 2,
        compiler_params=_params(2),
        name="hyena_spectrum",
    )(cm, sm, fsum, fdif)


HYENA_CHUNK = 256


def _hyena_kernel(v_ref, g0_ref, g1_ref, cwv_ref, cw0_ref, cw1_ref, cbv_ref, cb0_ref, cb1_ref,
                  cm_ref, sm_ref, kr_ref, ki_ref, kn_ref, fb_ref, z_ref,
                  u_s, ub_s, zc_s, zs_s, *, n):
    ct = v_ref.shape[-1]
    tc = min(HYENA_CHUNK, n)
    nchunks = n // tc
    zero_row = jnp.zeros((1, ct), F32)

    def alt_sign(s):
        row = lax.broadcasted_iota(jnp.int32, (tc, ct), 0) + s
        return jnp.where((row & 1) == 0, 1.0, -1.0)

    def put_u(s, u, acc):
        u_s[pl.ds(s, tc), :] = u
        ub_s[pl.ds(s, tc), :] = u.astype(BF16)
        return acc + jnp.sum(u * alt_sign(s), axis=0, keepdims=True)

    def first(ci, acc):
        s = pl.multiple_of(ci * tc, tc)
        return put_u(s, _dwconv_chunk(v_ref, s, tc, n, cwv_ref, cbv_ref, 1), acc)

    nyq_sum = lax.fori_loop(0, nchunks, first, zero_row)

    for o, (g_ref, cw_ref, cb_ref) in enumerate(((g0_ref, cw0_ref, cb0_ref), (g1_ref, cw1_ref, cb1_ref))):
        last = o == HYENA_ORDER - 1

        def spectrum(ci, carry, o=o):
            rows = pl.ds(pl.multiple_of(ci * tc, tc), tc)
            xre = jnp.dot(cm_ref[rows, :], ub_s[...], preferred_element_type=F32)
            xim = jnp.dot(sm_ref[rows, :], ub_s[...], preferred_element_type=F32)
            kr = kr_ref[o, rows, :]
            ki = ki_ref[o, rows, :]
            zc_s[rows, :] = (xre * kr + xim * ki).astype(BF16)
            zs_s[rows, :] = (xim * kr - xre * ki).astype(BF16)
            return carry

        lax.fori_loop(0, nchunks, spectrum, 0)
        nyq = nyq_sum * kn_ref[o]

        def inverse(ci, acc, o=o, g_ref=g_ref, cw_ref=cw_ref, cb_ref=cb_ref, last=last, nyq=nyq):
            s = pl.multiple_of(ci * tc, tc)
            rows = pl.ds(s, tc)
            y = (jnp.dot(cm_ref[rows, :], zc_s[...], preferred_element_type=F32)
                 + jnp.dot(sm_ref[rows, :], zs_s[...], preferred_element_type=F32))
            y = y + alt_sign(s) * nyq + u_s[rows, :] * fb_ref[o]
            z = _dwconv_chunk(g_ref, s, tc, n, cw_ref, cb_ref, 1) * y
            if last:
                z_ref[rows, :] = z.astype(z_ref.dtype)
                return acc
            return put_u(s, z, acc)

        nyq_sum = lax.fori_loop(0, nchunks, inverse, zero_row)


def _hyena(proj, hy_col0, conv_w, conv_b, cm_bf16, sm_bf16, kr, ki, kn, filt_bias, ct=256,
           out_dtype=F32):
    b, n, _ = proj.shape
    hw = kr.shape[-1]
    ct = min(ct, hw)
    nct = hw // ct
    base = hy_col0 // ct
    k_taps = conv_w.shape[0]

    def seq(j):
        return pl.BlockSpec((None, n, ct), lambda c, i: (i, 0, base + j * nct + c))

    def cw(j):
        return pl.BlockSpec((k_taps, ct), lambda c, i: (0, j * nct + c))

    def cb(j):
        return pl.BlockSpec((1, ct), lambda c, i: (0, j * nct + c))

    kspec = _single((HYENA_ORDER, n, ct), lambda c, i: (0, 0, c))
    rowspec = pl.BlockSpec((HYENA_ORDER, 1, ct), lambda c, i: (0, 0, c))
    conv_b2 = conv_b.reshape(1, -1)
    return pl.pallas_call(
        functools.partial(_hyena_kernel, n=n),
        grid=(nct, b),
        in_specs=[seq(0), seq(1), seq(2), cw(0), cw(1), cw(2), cb(0), cb(1), cb(2),
                  _single((n, n), lambda c, i: (0, 0)), _single((n, n), lambda c, i: (0, 0)),
                  kspec, kspec, rowspec, rowspec],
        out_specs=pl.BlockSpec((None, n, ct), lambda c, i: (i, 0, c)),
        out_shape=jax.ShapeDtypeStruct((b, n, hw), out_dtype),
        scratch_shapes=[pltpu.VMEM((n, ct), F32), pltpu.VMEM((n, ct), BF16),
                        pltpu.VMEM((n, ct), BF16), pltpu.VMEM((n, ct), BF16)],
        compiler_params=_params(2),
        name="hyena",
    )(proj, proj, proj, conv_w, conv_w, conv_w, conv_b2, conv_b2, conv_b2,
      cm_bf16, sm_bf16, kr, ki, kn, filt_bias.reshape(HYENA_ORDER, 1, hw))


def _outproj_kernel(*refs, n_act, has_bias):
    acts = refs[:n_act]
    (w_ref, x_ref, g1_ref, ng_ref, sh_ref, sc_ref, rt_ref) = refs[n_act:n_act + 7]
    rest = refs[n_act + 7:]
    x1_ref, h2_ref, lg_ref = rest[-3:]
    y = None
    k0 = 0
    for a_ref in acts:
        k = a_ref.shape[-1]
        part = jnp.dot(a_ref[...].astype(BF16), w_ref[k0:k0 + k, :], preferred_element_type=F32)
        y = part if y is None else y + part
        k0 += k
    if has_bias:
        y = y + rest[0][...]
    x1 = x_ref[...] + g1_ref[...] * y
    x1_ref[...] = x1
    h2 = _norm_mod(x1, ng_ref[...], sh_ref[...], sc_ref[...])
    h2_ref[...] = h2.astype(h2_ref.dtype)
    lg_ref[...] = lax.dot_general(rt_ref[...], h2, (((1,), (1,)), ((), ())),
                                  precision=HIGHEST, preferred_element_type=F32)


def _outproj(acts, w_bf16, bias, x, mods, norm_g, router, tn=256):
    b, n, d = x.shape
    e = router.shape[1]
    tn = min(tn, n)
    row = lambda j: pl.BlockSpec((None, 1, d), lambda i, t: (i * N_MOD + j, 0, 0))
    in_specs = [pl.BlockSpec((None, tn, a.shape[-1]), lambda i, t: (i, t, 0)) for a in acts]
    in_specs += [_single(w_bf16.shape, lambda i, t: (0, 0)),
                 pl.BlockSpec((None, tn, d), lambda i, t: (i, t, 0)),
                 row(2), pl.BlockSpec((1, d), lambda i, t: (0, 0)), row(3), row(4),
                 pl.BlockSpec((e, d), lambda i, t: (0, 0))]
    args = list(acts) + [w_bf16, x, mods, norm_g.reshape(1, d), mods, mods, router.T]
    if bias is not None:
        in_specs.append(pl.BlockSpec((1, d), lambda i, t: (0, 0)))
        args.append(bias.reshape(1, d))
    return pl.pallas_call(
        functools.partial(_outproj_kernel, n_act=len(acts), has_bias=bias is not None),
        grid=(b, n // tn),
        in_specs=in_specs,
        out_specs=[pl.BlockSpec((None, tn, d), lambda i, t: (i, t, 0)),
                   pl.BlockSpec((None, tn, d), lambda i, t: (i, t, 0)),
                   pl.BlockSpec((None, e, tn), lambda i, t: (i, 0, t))],
        out_shape=[jax.ShapeDtypeStruct((b, n, d), F32),
                   jax.ShapeDtypeStruct((b, n, d), BF16),
                   jax.ShapeDtypeStruct((b, e, n), F32)],
        compiler_params=_params(2),
        name="outproj_router",
    )(*args)


def _cumsum_lanes(x):
    n = x.shape[-1]
    lane = lax.broadcasted_iota(jnp.int32, x.shape, 1)
    s = 1
    while s < n:
        x = x + jnp.where(lane >= s, pltpu.roll(x, s, 1), 0)
        s *= 2
    return x


def _route_kernel(lg_ref, pos_ref, gate_ref, *, cap):
    lg = lg_ref[...]
    m = jnp.max(lg, axis=0, keepdims=True)
    ex = jnp.exp(lg - m)
    probs = ex / jnp.sum(ex, axis=0, keepdims=True)
    thr = jnp.zeros((lg.shape[0], 1), jnp.int32)
    for bit in range(30, -1, -1):
        cand = thr | (1 << bit)
        cnt = jnp.sum(jnp.where(probs >= pltpu.bitcast(cand, F32), 1.0, 0.0), axis=1, keepdims=True)
        thr = jnp.where(cnt >= cap, cand, thr)
    gt = probs >= pltpu.bitcast(thr + 1, F32)
    eq = jnp.logical_and(probs >= pltpu.bitcast(thr, F32), jnp.logical_not(gt))
    need = cap - jnp.sum(jnp.where(gt, 1.0, 0.0), axis=1, keepdims=True)
    eq_rank = _cumsum_lanes(jnp.where(eq, 1.0, 0.0))
    sel = jnp.where(gt, 1.0, jnp.where(eq, jnp.where(eq_rank <= need, 1.0, 0.0), 0.0))
    slot = _cumsum_lanes(sel) - 1.0
    pos_ref[...] = jnp.where(sel > 0.0, slot, -1.0).astype(jnp.int32)
    gate_ref[...] = jnp.where(sel > 0.0, probs, 0.0)


def _route(logits, cap):
    b, e, n = logits.shape
    spec = pl.BlockSpec((None, e, n), lambda i: (i, 0, 0))
    return pl.pallas_call(
        functools.partial(_route_kernel, cap=cap),
        grid=(b,),
        in_specs=[spec],
        out_specs=[spec, spec],
        out_shape=[jax.ShapeDtypeStruct((b, e, n), jnp.int32), jax.ShapeDtypeStruct((b, e, n), F32)],
        compiler_params=_params(1),
        name="route",
    )(logits)


MOE_ROW_CHUNK = 512


def _moe_kernel(h_ref,pos_ref, gate_ref, x_ref, g2_ref, w1_ref, w3_ref, w2_ref, *rest, cap, final):
    o_ref = rest[-1]
    e = pl.program_id(1)
    n = h_ref.shape[0]

    @pl.when(e == 0)
    def _():
        o_ref[...] = x_ref[...]

    pos = pos_ref[pl.ds(e, 1), :]
    gate = gate_ref[pl.ds(e, 1), :]
    hit = lax.broadcasted_iota(jnp.int32, (cap, n), 0) == pos
    onehot = jnp.where(hit, 1.0, 0.0).astype(BF16)
    xs = jnp.dot(onehot, h_ref[...], preferred_element_type=F32).astype(BF16)
    a = jnp.dot(xs, w1_ref[...], preferred_element_type=F32)
    g = jnp.dot(xs, w3_ref[...], preferred_element_type=F32)
    y = jnp.dot((_silu(a) * g).astype(BF16), w2_ref[...], preferred_element_type=F32)
    slot_gate = jnp.sum(jnp.where(hit, gate, 0.0), axis=1, keepdims=True)
    ys = (y * slot_gate * g2_ref[...]).astype(BF16)
    tr = min(MOE_ROW_CHUNK, n)
    for r0 in range(0, n, tr):
        o_ref[r0:r0 + tr, :] += lax.dot_general(onehot[:, r0:r0 + tr], ys, (((0,), (0,)), ((), ())),
                                                preferred_element_type=F32)

    if final:
        @pl.when(e == pl.num_programs(1) - 1)
        def _():
            def norm_rows(ci, carry):
                rows = pl.ds(pl.multiple_of(ci * tr, tr), tr)
                xf = o_ref[rows, :]
                ms = jnp.mean(xf * xf, axis=-1, keepdims=True)
                o_ref[rows, :] = xf * lax.rsqrt(ms + EPS) * rest[0][...]
                return carry
            lax.fori_loop(0, n // tr, norm_rows, 0)


def _moe(h2, pos, gate, x1, mods, w1, w3, w2, final_g=None):
    b, n, d = x1.shape
    e, _, f = w1.shape
    cap = EC_CAPACITY * n // e
    final = final_g is not None
    in_specs = [_single((None, n, d), lambda i, k: (i, 0, 0)),
                pl.BlockSpec((None, e, n), lambda i, k: (i, 0, 0)),
                pl.BlockSpec((None, e, n), lambda i, k: (i, 0, 0)),
                _single((None, n, d), lambda i, k: (i, 0, 0)),
                pl.BlockSpec((None, 1, d), lambda i, k: (i * N_MOD + 5, 0, 0)),
                pl.BlockSpec((None, d, f), lambda i, k: (k, 0, 0)),
                pl.BlockSpec((None, d, f), lambda i, k: (k, 0, 0)),
                pl.BlockSpec((None, f, d), lambda i, k: (k, 0, 0))]
    args = [h2, pos, gate, x1, mods, w1, w3, w2]
    if final:
        in_specs.append(pl.BlockSpec((1, d), lambda i, k: (0, 0)))
        args.append(final_g.reshape(1, d))
    return pl.pallas_call(
        functools.partial(_moe_kernel, cap=cap, final=final),
        grid=(b, e),
        in_specs=in_specs,
        out_specs=pl.BlockSpec((None, n, d), lambda i, k: (i, 0, 0)),
        out_shape=jax.ShapeDtypeStruct((b, n, d), F32),
        compiler_params=_params(2),
        name="moe",
    )(*args)


def _vconv_kernel(u_ref, w_ref, b_ref, lg_ref, lb_ref, o_ref, pad_s, *, n, k_taps):
    d = u_ref.shape[-1]
    halo = (k_taps - 1) // 2 * GRID_W
    tail = k_taps // 2 * GRID_W
    pad_s[0:halo, :] = jnp.zeros((halo, d), F32)
    pad_s[halo + n:halo + n + tail, :] = jnp.zeros((tail, d), F32)
    pad_s[halo:halo + n, :] = u_ref[...]

    def body(r, carry):
        acc = jnp.zeros((GRID_W, d), F32)
        for k in range(k_taps):
            s = pl.multiple_of((r + k) * GRID_W, GRID_W)
            acc = acc + pad_s[pl.ds(s, GRID_W), :] * w_ref[k:k + 1, :]
        acc = acc + b_ref[...]
        mu = jnp.mean(acc, axis=-1, keepdims=True)
        xc = acc - mu
        var = jnp.mean(xc * xc, axis=-1, keepdims=True)
        y = xc * lax.rsqrt(var + EPS) * lg_ref[...] + lb_ref[...]
        o_ref[pl.ds(pl.multiple_of(r * GRID_W, GRID_W), GRID_W), :] = _silu(y).astype(o_ref.dtype)
        return carry

    lax.fori_loop(0, n // GRID_W, body, 0)


def _vconv_ln_silu(u, dw_w, dw_b, ln_g, ln_b):
    b, n, d = u.shape
    k_taps = dw_w.shape[0]
    full = lambda shape: pl.BlockSpec(shape, lambda i: tuple(0 for _ in shape))
    return pl.pallas_call(
        functools.partial(_vconv_kernel, n=n, k_taps=k_taps),
        grid=(b,),
        in_specs=[pl.BlockSpec((None, n, d), lambda i: (i, 0, 0)),
                  full((k_taps, d)), full((1, d)), full((1, d)), full((1, d))],
        out_specs=pl.BlockSpec((None, n, d), lambda i: (i, 0, 0)),
        out_shape=jax.ShapeDtypeStruct((b, n, d), BF16),
        scratch_shapes=[pltpu.VMEM((n + (k_taps - 1) * GRID_W, d), F32)],
        compiler_params=_params(1),
        name="vconv_ln_silu",
    )(u, dw_w, dw_b.reshape(1, d), ln_g.reshape(1, d), ln_b.reshape(1, d))


def kernel(x, c, ctx, c_ctx, norm1_g, norm2_g, w_mod, b_mod, w_in, conv_a_w, conv_a_b, lru_wr, lru_br, lru_wi, lru_bi, lru_lam, conv_b_w, conv_b_b, filt_w1, filt_b1, filt_w2, filt_b2, filt_w3, filt_b3, filt_freq, filt_bias, w_out, cf_w1, cf_b1, cf_dw_w, cf_dw_b, cf_ln_g, cf_ln_b, cf_w2, cf_b2, router, exp_w1, exp_w3, exp_w2, final_g):
    bsz, n, d = x.shape
    depth = w_mod.shape[0]
    assert depth == 2 and w_in.shape[0] == 1 and cf_w1.shape[0] == 1
    rw = conv_a_w.shape[-1]
    hw = filt_bias.shape[-1]
    e = router.shape[-1]
    cap = EC_CAPACITY * n // e

    rows = -(-(bsz + 1) // SUBLANES) * SUBLANES
    cvec = jnp.concatenate([c, c_ctx[None], jnp.zeros((rows - bsz - 1, d), F32)], axis=0)
    mods_all = _adaln(cvec, w_mod, b_mod)
    mods = [mods_all[l].reshape(rows * N_MOD, 1, d) for l in range(depth)]

    w_in0 = w_in[0].astype(BF16)
    wg, bg = _lru_gate_weights(lru_wr[0], lru_wi[0], lru_br[0], lru_bi[0])
    ctx_xa = _norm_matmul(ctx, norm1_g[0], mods[0], lambda i: bsz, 0, 1, w_in0[:, rw:2 * rw])
    h0_ctx = jnp.zeros((bsz, 2, rw), F32)
    _, ctx_state = _rglru(ctx_xa, 0, None, 0, conv_a_w[0], conv_a_b[0], wg, bg, lru_lam[0], h0_ctx)
    proj = _norm_matmul(x, norm1_g[0], mods[0], lambda i: i, 0, 1, w_in0)
    y_a, _ = _rglru(proj, 1, proj, 0, conv_a_w[0], conv_a_b[0], wg, bg, lru_lam[0], ctx_state)

    fsum, fdif, kn = _hyena_filters(n, filt_w1[0], filt_b1[0], filt_w2[0], filt_b2[0],
                                    filt_w3[0], filt_b3[0], filt_freq[0])
    cm, sm = (jnp.asarray(t) for t in _dft_tables(n))
    kr, ki = _hyena_spectrum(cm, sm, fsum, fdif)
    z = _hyena(proj, 2 * rw, conv_b_w[0], conv_b_b[0], cm.astype(BF16), sm.astype(BF16),
               kr, ki, kn, filt_bias[0])

    x1, h2, logits = _outproj([y_a, z], w_out[0].astype(BF16), None, x, mods[0], norm2_g[0], router[0])
    pos, gate = _route(logits, cap)
    xl = _moe(h2, pos, gate, x1, mods[0], exp_w1[0].astype(BF16), exp_w3[0].astype(BF16),
              exp_w2[0].astype(BF16))

    u = _norm_matmul(xl, norm1_g[1], mods[1], lambda i: i, 0, 1, cf_w1[0].astype(BF16),
                     bias=cf_b1[0], glu=True)
    v = _vconv_ln_silu(u, cf_dw_w[0], cf_dw_b[0], cf_ln_g[0], cf_ln_b[0])
    x1, h2, logits = _outproj([v], cf_w2[0].astype(BF16), cf_b2[0], xl, mods[1], norm2_g[1], router[1])
    pos, gate = _route(logits, cap)
    return _moe(h2, pos, gate, x1, mods[1], exp_w1[1].astype(BF16), exp_w3[1].astype(BF16),
                exp_w2[1].astype(BF16), final_g=final_g)
```

```python
import functools
import math

import numpy as np
import jax
import jax.numpy as jnp
from jax import lax
from jax.experimental import pallas as pl
from jax.experimental.pallas import tpu as pltpu

F32 = jnp.float32
BF16 = jnp.bfloat16
HIGHEST = lax.Precision.HIGHEST

EPS = 1e-6
GRID_W = 64
N_MOD = 6
LRU_HEADS = 8
LRU_C = 8.0
HYENA_ORDER = 2
HYENA_DECAY_TARGET = 1e-2
HYENA_FAST_PCT = 0.3
HYENA_SLOW_PCT = 1.5
HYENA_SHIFT = 0.05
EC_CAPACITY = 2

V7X_VMEM_BYTES = 64 * 1024 * 1024
VMEM_LIMIT = V7X_VMEM_BYTES - 8 * 1024 * 1024
SUBLANES = 8


def _params(n_axes):
    return pltpu.CompilerParams(dimension_semantics=("arbitrary",) * n_axes,
                                vmem_limit_bytes=VMEM_LIMIT)


def _single(block_shape, index_map):
    return pl.BlockSpec(block_shape, index_map, pipeline_mode=pl.Buffered(1))


def _silu(x):
    return x * jax.nn.sigmoid(x)


def _norm_mod(xf, g, shift, scale):
    ms = jnp.mean(xf * xf, axis=-1, keepdims=True)
    y = xf * lax.rsqrt(ms + EPS) * g
    return y * (1.0 + scale) + shift


def _adaln_kernel(c_ref, w_ref, b_ref, o_ref):
    s = _silu(c_ref[...])
    o_ref[...] = jnp.dot(s, w_ref[...], precision=HIGHEST, preferred_element_type=F32) + b_ref[...]


def _adaln(cvec, w_mod, b_mod, tc=512):
    depth, d, dm = w_mod.shape
    r = cvec.shape[0]
    return pl.pallas_call(
        _adaln_kernel,
        grid=(depth, dm // tc),
        in_specs=[pl.BlockSpec((r, d), lambda l, j: (0, 0)),
                  pl.BlockSpec((None, d, tc), lambda l, j: (l, 0, j)),
                  pl.BlockSpec((None, 1, tc), lambda l, j: (l, 0, j))],
        out_specs=pl.BlockSpec((None, r, tc), lambda l, j: (l, 0, j)),
        out_shape=jax.ShapeDtypeStruct((depth, r, dm), F32),
        compiler_params=_params(2),
        name="adaln",
    )(cvec, w_mod, b_mod.reshape(depth, 1, dm))


def _norm_matmul_kernel(x_ref, g_ref, sh_ref, sc_ref, w_ref, *rest, has_bias, glu):
    o_ref = rest[-1]
    h = _norm_mod(x_ref[...], g_ref[...], sh_ref[...], sc_ref[...])
    y = jnp.dot(h.astype(BF16), w_ref[...], preferred_element_type=F32)
    if has_bias:
        y = y + rest[0][...]
    if glu:
        half = y.shape[-1] // 2
        y = y[:, :half] * jax.nn.sigmoid(y[:, half:])
    o_ref[...] = y.astype(o_ref.dtype)


def _norm_matmul(x, g, mods, mod_row, shift_j, scale_j, w_bf16, bias=None, glu=False, tn=512,
                 out_dtype=F32):
    b, n, d = x.shape
    nout = w_bf16.shape[1]
    tn = min(tn, n)
    out_w = nout // 2 if glu else nout
    in_specs = [pl.BlockSpec((None, tn, d), lambda i, j: (i, j, 0)),
                pl.BlockSpec((1, d), lambda i, j: (0, 0)),
                pl.BlockSpec((None, 1, d), lambda i, j: (mod_row(i) * N_MOD + shift_j, 0, 0)),
                pl.BlockSpec((None, 1, d), lambda i, j: (mod_row(i) * N_MOD + scale_j, 0, 0)),
                _single((d, nout), lambda i, j: (0, 0))]
    args = [x, g.reshape(1, d), mods, mods, w_bf16]
    if bias is not None:
        in_specs.append(pl.BlockSpec((1, nout), lambda i, j: (0, 0)))
        args.append(bias.reshape(1, nout))
    return pl.pallas_call(
        functools.partial(_norm_matmul_kernel, has_bias=bias is not None, glu=glu),
        grid=(b, n // tn),
        in_specs=in_specs,
        out_specs=pl.BlockSpec((None, tn, out_w), lambda i, j: (i, j, 0)),
        out_shape=jax.ShapeDtypeStruct((b, n, out_w), out_dtype),
        compiler_params=_params(2),
        name="norm_matmul",
    )(*args)


HALO = 16


def _shift_rows(cur, k, prev, nxt, has_prev, has_next):
    t = cur.shape[0]
    row = lax.broadcasted_iota(jnp.int32, cur.shape, 0)
    if k < 0:
        out = pltpu.roll(cur, -k, 0)
        for r in range(-k):
            src = prev[HALO + k + r:HALO + k + r + 1, :]
            fill = jnp.where(has_prev, src, 0.0)
            out = jnp.where(row == r, fill, out)
    else:
        out = pltpu.roll(cur, t - k, 0)
        for r in range(k):
            src = nxt[r:r + 1, :]
            fill = jnp.where(has_next, src, 0.0)
            out = jnp.where(row == t - k + r, fill, out)
    return out


def _dwconv_chunk(x_ref, s, t, n, w_ref, b_ref, pad_left):
    cur = x_ref[pl.ds(s, t), :].astype(F32)
    prev = x_ref[pl.ds(pl.multiple_of(jnp.maximum(s - HALO, 0), HALO), HALO), :].astype(F32)
    nxt = x_ref[pl.ds(pl.multiple_of(jnp.minimum(s + t, n - HALO), HALO), HALO), :].astype(F32)
    has_prev = s > 0
    has_next = s + t < n
    k_taps = w_ref.shape[0]
    acc = None
    for k in range(k_taps):
        off = k - pad_left
        xs = cur if off == 0 else _shift_rows(cur, off, prev, nxt, has_prev, has_next)
        term = xs * w_ref[k:k + 1, :]
        acc = term if acc is None else acc + term
    return acc + b_ref[...]


GATE_CHUNK = 256
SCAN_CHUNK = 64


def _scan_rows(a, b, reverse):
    t = a.shape[0]
    row = lax.broadcasted_iota(jnp.int32, a.shape, 0)
    s = 1
    while s < t:
        if reverse:
            valid = row < t - s
            sh = t - s
        else:
            valid = row >= s
            sh = s
        a_s = jnp.where(valid, pltpu.roll(a, sh, 0), 1.0)
        b_s = jnp.where(valid, pltpu.roll(b, sh, 0), 0.0)
        b = a * b_s + b
        a = a * a_s
        s *= 2
    return a, b


def _gelu_tanh(x):
    return 0.5 * x * (1.0 + jnp.tanh(math.sqrt(2.0 / math.pi) * (x + 0.044715 * (x * x * x))))


def _rglru_kernel(*refs, n, with_gate):
    if with_gate:
        (xa_ref, gate_ref, cw_ref, cb_ref, wg_ref, bg_ref, lam_ref, h0_ref,
         y_ref, hl_ref, af_s, uf_s, ab_s, ub_s) = refs
    else:
        (xa_ref, cw_ref, cb_ref, wg_ref, bg_ref, lam_ref, h0_ref,
         hl_ref, af_s, uf_s, ab_s, ub_s) = refs
    w = xa_ref.shape[-1]
    tg = min(GATE_CHUNK, n)
    ts = min(SCAN_CHUNK, n)
    lam = lam_ref[...]
    sp = jnp.maximum(-lam, 0.0) + jnp.log(1.0 + jnp.exp(-jnp.abs(lam)))

    def gates(ci, carry):
        s = pl.multiple_of(ci * tg, tg)
        xa = _dwconv_chunk(xa_ref, s, tg, n, cw_ref, cb_ref, 1)
        z = jnp.dot(xa.astype(BF16), wg_ref[...], preferred_element_type=F32) + bg_ref[...]
        for d, (a_s, u_s) in enumerate(((af_s, uf_s), (ab_s, ub_s))):
            r = jax.nn.sigmoid(z[:, (2 * d) * w:(2 * d + 1) * w])
            i = jax.nn.sigmoid(z[:, (2 * d + 1) * w:(2 * d + 2) * w])
            log_a = (-LRU_C) * r * sp[d:d + 1, :]
            a_s[pl.ds(s, tg), :] = jnp.exp(log_a)
            u_s[pl.ds(s, tg), :] = jnp.sqrt(1.0 - jnp.exp(2.0 * log_a)) * i * xa
        return carry

    lax.fori_loop(0, n // tg, gates, 0)

    nc = n // ts

    def fwd(ci, carry):
        s = pl.multiple_of(ci * ts, ts)
        a_c, h_l = _scan_rows(af_s[pl.ds(s, ts), :], uf_s[pl.ds(s, ts), :], False)
        h = h_l + a_c * carry
        uf_s[pl.ds(s, ts), :] = h
        return h[ts - 1:ts, :]

    hf_last = lax.fori_loop(0, nc, fwd, h0_ref[0:1, :])

    def bwd(ci, carry):
        s = pl.multiple_of((nc - 1 - ci) * ts, ts)
        a_c, h_l = _scan_rows(ab_s[pl.ds(s, ts), :], ub_s[pl.ds(s, ts), :], True)
        h = h_l + a_c * carry
        if with_gate:
            hsum = uf_s[pl.ds(s, ts), :] + h
            gate = gate_ref[pl.ds(s, ts), :].astype(F32)
            y_ref[pl.ds(s, ts), :] = (_gelu_tanh(gate) * hsum).astype(y_ref.dtype)
        return h[0:1, :]

    hb_last = lax.fori_loop(0, nc, bwd, h0_ref[1:2, :])
    hl_ref[0:1, :] = hf_last
    hl_ref[1:2, :] = hb_last


def _rglru(xa_src, xa_col, gate_src, gate_col, conv_w, conv_b, wg_bf16, bg, lam, h0, out_dtype=F32):
    b, n, _ = xa_src.shape
    w = conv_w.shape[1]
    with_gate = gate_src is not None
    seq = lambda col: pl.BlockSpec((None, n, w), lambda i: (i, 0, col))
    full = lambda shape: pl.BlockSpec(shape, lambda i: tuple(0 for _ in shape))
    in_specs = [seq(xa_col)]
    args = [xa_src]
    if with_gate:
        in_specs.append(seq(gate_col))
        args.append(gate_src)
    in_specs += [full(conv_w.shape), full((1, w)), _single(wg_bf16.shape, lambda i: (0, 0)),
                 full((1, 4 * w)), full((2, w)), pl.BlockSpec((None, 2, w), lambda i: (i, 0, 0))]
    args += [conv_w, conv_b.reshape(1, w), wg_bf16, bg.reshape(1, 4 * w), lam, h0]
    out_specs = [pl.BlockSpec((None, 2, w), lambda i: (i, 0, 0))]
    out_shape = [jax.ShapeDtypeStruct((b, 2, w), F32)]
    if with_gate:
        out_specs.insert(0, pl.BlockSpec((None, n, w), lambda i: (i, 0, 0)))
        out_shape.insert(0, jax.ShapeDtypeStruct((b, n, w), out_dtype))
    outs = pl.pallas_call(
        functools.partial(_rglru_kernel, n=n, with_gate=with_gate),
        grid=(b,),
        in_specs=in_specs,
        out_specs=out_specs,
        out_shape=out_shape,
        scratch_shapes=[pltpu.VMEM((n, w), F32)] * 4,
        compiler_params=_params(1),
        name="rglru_gated" if with_gate else "rglru_state",
    )(*args)
    return (outs[0], outs[1]) if with_gate else (None, outs[0])


def _lru_gate_weights(lru_wr, lru_wi, lru_br, lru_bi):
    def dense(wh):
        heads, hd, _ = wh.shape
        eye = jnp.eye(heads, dtype=wh.dtype)
        return jnp.einsum('hij,hg->higj', wh, eye).reshape(heads * hd, heads * hd)
    wg = jnp.concatenate([dense(lru_wr[0]), dense(lru_wi[0]), dense(lru_wr[1]), dense(lru_wi[1])], axis=1)
    bg = jnp.concatenate([lru_br[0], lru_bi[0], lru_br[1], lru_bi[1]], axis=0)
    return wg.astype(BF16), bg


@functools.lru_cache(maxsize=None)
def _dft_tables(n):
    m = 2 * n
    k = (np.arange(n, dtype=np.int64)[:, None] * np.arange(n, dtype=np.int64)[None, :]) % m
    ang = 2.0 * np.pi * np.arange(m, dtype=np.float64) / m
    cm = np.cos(ang)[k].astype(np.float32)
    sm = np.sin(ang)[k].astype(np.float32)
    return cm, sm


@functools.lru_cache(maxsize=None)
def _filter_features(n, emb):
    t = np.linspace(0.0, 1.0, n, dtype=np.float32)[:, None]
    bands = (emb - 1) // 2
    w = (np.float32(2.0 * math.pi / n) * np.arange(n, dtype=np.float32))[:, None]
    f = np.linspace(1e-4, bands - 1, bands, dtype=np.float32)[None, :]
    z = np.concatenate([t, np.cos(f * w), -np.sin(f * w)], axis=-1).astype(np.float32)
    return z


@functools.lru_cache(maxsize=None)
def _hyena_deltas(width):
    d = np.abs(np.linspace(math.log(HYENA_DECAY_TARGET) / HYENA_SLOW_PCT,
                           math.log(HYENA_DECAY_TARGET) / HYENA_FAST_PCT, width, dtype=np.float32))
    return d.reshape(1, width).astype(np.float32)


def _filter_kernel(z_ref, t_ref, w1_ref, b1_ref, w2_ref, b2_ref, w3_ref, b3_ref, fr_ref, dl_ref,
                   fsum_ref, fdif_ref, kn_ref, *, n, tn):
    j = pl.program_id(0)
    hw = dl_ref.shape[-1]
    hid = jnp.sin(fr_ref[0:1, :] * (jnp.dot(z_ref[...], w1_ref[...], precision=HIGHEST,
                                            preferred_element_type=F32) + b1_ref[...]))
    hid = jnp.sin(fr_ref[1:2, :] * (jnp.dot(hid, w2_ref[...], precision=HIGHEST,
                                            preferred_element_type=F32) + b2_ref[...]))
    filt = jnp.dot(hid, w3_ref[...], precision=HIGHEST, preferred_element_type=F32) + b3_ref[...]
    window = jnp.exp(-t_ref[...] * dl_ref[...]) + HYENA_SHIFT
    row = lax.broadcasted_iota(jnp.int32, (tn, hw), 0) + j * tn
    alt = jnp.where((row & 1) == 0, 1.0, -1.0)

    @pl.when(j == 0)
    def _():
        kn_ref[...] = jnp.zeros_like(kn_ref)

    for o in range(HYENA_ORDER):
        fwd = filt[:, (2 * o) * hw:(2 * o + 1) * hw] * window
        bwd = filt[:, (2 * o + 1) * hw:(2 * o + 2) * hw] * window
        bwd = jnp.where(row == 0, 0.0, bwd)
        fs = fwd + bwd
        fsum_ref[o] = fs
        fdif_ref[o] = bwd - fwd
        kn_ref[o] += jnp.sum(fs * alt, axis=0, keepdims=True) * (1.0 / (2 * n))


def _hyena_filters(n, filt_w1, filt_b1, filt_w2, filt_b2, filt_w3, filt_b3, filt_freq, tn=256):
    emb, hidden = filt_w1.shape
    hw = filt_w3.shape[1] // (2 * HYENA_ORDER)
    tn = min(tn, n)
    z = jnp.asarray(_filter_features(n, emb))
    t = jnp.asarray(np.linspace(0.0, 1.0, n, dtype=np.float32)[:, None])
    dl = jnp.asarray(_hyena_deltas(hw))
    full = lambda shape: pl.BlockSpec(shape, lambda j: tuple(0 for _ in shape))
    return pl.pallas_call(
        functools.partial(_filter_kernel, n=n, tn=tn),
        grid=(n // tn,),
        in_specs=[pl.BlockSpec((tn, emb), lambda j: (j, 0)),
                  pl.BlockSpec((tn, 1), lambda j: (j, 0)),
                  full((emb, hidden)), full((1, hidden)), full((hidden, hidden)), full((1, hidden)),
                  full((hidden, 2 * HYENA_ORDER * hw)), full((1, 2 * HYENA_ORDER * hw)),
                  full((2, hidden)), full((1, hw))],
        out_specs=[pl.BlockSpec((HYENA_ORDER, tn, hw), lambda j: (0, j, 0)),
                   pl.BlockSpec((HYENA_ORDER, tn, hw), lambda j: (0, j, 0)),
                   pl.BlockSpec((HYENA_ORDER, 1, hw), lambda j: (0, 0, 0))],
        out_shape=[jax.ShapeDtypeStruct((HYENA_ORDER, n, hw), F32),
                   jax.ShapeDtypeStruct((HYENA_ORDER, n, hw), F32),
                   jax.ShapeDtypeStruct((HYENA_ORDER, 1, hw), F32)],
        compiler_params=_params(1),
        name="hyena_filter",
    )(z, t, filt_w1, filt_b1.reshape(1, -1), filt_w2, filt_b2.reshape(1, -1),
      filt_w3, filt_b3.reshape(1, -1), filt_freq, dl)


def _spectrum_kernel(cm_ref, sm_ref, fsum_ref, fdif_ref, kr_ref, ki_ref, *, n, tm):
    j = pl.program_id(1)
    row = lax.broadcasted_iota(jnp.int32, kr_ref.shape, 0) + j * tm
    scale = jnp.where(row == 0, 1.0 / (2 * n), 2.0 / (2 * n))
    kr_ref[...] = scale * jnp.dot(cm_ref[...], fsum_ref[...], precision=HIGHEST,
                                  preferred_element_type=F32)
    ki_ref[...] = scale * jnp.dot(sm_ref[...], fdif_ref[...], precision=HIGHEST,
                                  preferred_element_type=F32)


def _hyena_spectrum(cm, sm, fsum, fdif, tm=256):
    order, n, hw = fsum.shape
    tm = min(tm, n)
    return pl.pallas_call(
        functools.partial(_spectrum_kernel, n=n, tm=tm),
        grid=(order, n // tm),
        in_specs=[pl.BlockSpec((tm, n), lambda o, j: (j, 0)),
                  pl.BlockSpec((tm, n), lambda o, j: (j, 0)),
                  pl.BlockSpec((None, n, hw), lambda o, j: (o, 0, 0)),
                  pl.BlockSpec((None, n, hw), lambda o, j: (o, 0, 0))],
        out_specs=[pl.BlockSpec((None, tm, hw), lambda o, j: (o, j, 0)),
                   pl.BlockSpec((None, tm, hw), lambda o, j: (o, j, 0))],
        out_shape=[jax.ShapeDtypeStruct((order, n, hw), F32)] * 2,
        compiler_params=_params(2),
        name="hyena_spectrum",
    )(cm, sm, fsum, fdif)


HYENA_CHUNK = 512


def _hyena_kernel(v_ref, g0_ref, g1_ref, cwv_ref, cw0_ref, cw1_ref, cbv_ref, cb0_ref, cb1_ref,
                  cm_ref, sm_ref, kr_ref, ki_ref, kn_ref, fb_ref, z_ref,
                  u_s, ub_s, zc_s, zs_s, *, n):
    ct = v_ref.shape[-1]
    tc = min(HYENA_CHUNK, n)
    nchunks = n // tc
    zero_row = jnp.zeros((1, ct), F32)

    def alt_sign(s):
        row = lax.broadcasted_iota(jnp.int32, (tc, ct), 0) + s
        return jnp.where((row & 1) == 0, 1.0, -1.0)

    def put_u(s, u, acc):
        u_s[pl.ds(s, tc), :] = u
        ub_s[pl.ds(s, tc), :] = u.astype(BF16)
        return acc + jnp.sum(u * alt_sign(s), axis=0, keepdims=True)

    def first(ci, acc):
        s = pl.multiple_of(ci * tc, tc)
        return put_u(s, _dwconv_chunk(v_ref, s, tc, n, cwv_ref, cbv_ref, 1), acc)

    nyq_sum = lax.fori_loop(0, nchunks, first, zero_row)

    for o, (g_ref, cw_ref, cb_ref) in enumerate(((g0_ref, cw0_ref, cb0_ref), (g1_ref, cw1_ref, cb1_ref))):
        last = o == HYENA_ORDER - 1

        def spectrum(ci, carry, o=o):
            rows = pl.ds(pl.multiple_of(ci * tc, tc), tc)
            xre = jnp.dot(cm_ref[rows, :], ub_s[...], preferred_element_type=F32)
            xim = jnp.dot(sm_ref[rows, :], ub_s[...], preferred_element_type=F32)
            kr = kr_ref[o, rows, :]
            ki = ki_ref[o, rows, :]
            zc_s[rows, :] = (xre * kr + xim * ki).astype(BF16)
            zs_s[rows, :] = (xim * kr - xre * ki).astype(BF16)
            return carry

        lax.fori_loop(0, nchunks, spectrum, 0)
        nyq = nyq_sum * kn_ref[o]

        def inverse(ci, acc, o=o, g_ref=g_ref, cw_ref=cw_ref, cb_ref=cb_ref, last=last, nyq=nyq):
            s = pl.multiple_of(ci * tc, tc)
            rows = pl.ds(s, tc)
            y = (jnp.dot(cm_ref[rows, :], zc_s[...], preferred_element_type=F32)
                 + jnp.dot(sm_ref[rows, :], zs_s[...], preferred_element_type=F32))
            y = y + alt_sign(s) * nyq + u_s[rows, :] * fb_ref[o]
            z = _dwconv_chunk(g_ref, s, tc, n, cw_ref, cb_ref, 1) * y
            if last:
                z_ref[rows, :] = z.astype(z_ref.dtype)
                return acc
            return put_u(s, z, acc)

        nyq_sum = lax.fori_loop(0, nchunks, inverse, zero_row)


def _hyena(proj, hy_col0, conv_w, conv_b, cm_bf16, sm_bf16, kr, ki, kn, filt_bias, ct=256,
           out_dtype=F32):
    b, n, _ = proj.shape
    hw = kr.shape[-1]
    ct = min(ct, hw)
    nct = hw // ct
    base = hy_col0 // ct
    k_taps = conv_w.shape[0]

    def seq(j):
        return pl.BlockSpec((None, n, ct), lambda c, i: (i, 0, base + j * nct + c))

    def cw(j):
        return pl.BlockSpec((k_taps, ct), lambda c, i: (0, j * nct + c))

    def cb(j):
        return pl.BlockSpec((1, ct), lambda c, i: (0, j * nct + c))

    kspec = _single((HYENA_ORDER, n, ct), lambda c, i: (0, 0, c))
    rowspec = pl.BlockSpec((HYENA_ORDER, 1, ct), lambda c, i: (0, 0, c))
    conv_b2 = conv_b.reshape(1, -1)
    return pl.pallas_call(
        functools.partial(_hyena_kernel, n=n),
        grid=(nct, b),
        in_specs=[seq(0), seq(1), seq(2), cw(0), cw(1), cw(2), cb(0), cb(1), cb(2),
                  _single((n, n), lambda c, i: (0, 0)), _single((n, n), lambda c, i: (0, 0)),
                  kspec, kspec, rowspec, rowspec],
        out_specs=pl.BlockSpec((None, n, ct), lambda c, i: (i, 0, c)),
        out_shape=jax.ShapeDtypeStruct((b, n, hw), out_dtype),
        scratch_shapes=[pltpu.VMEM((n, ct), F32), pltpu.VMEM((n, ct), BF16),
                        pltpu.VMEM((n, ct), BF16), pltpu.VMEM((n, ct), BF16)],
        compiler_params=_params(2),
        name="hyena",
    )(proj, proj, proj, conv_w, conv_w, conv_w, conv_b2, conv_b2, conv_b2,
      cm_bf16, sm_bf16, kr, ki, kn, filt_bias.reshape(HYENA_ORDER, 1, hw))


def _outproj_kernel(*refs, n_act, has_bias):
    acts = refs[:n_act]
    (w_ref, x_ref, g1_ref, ng_ref, sh_ref, sc_ref, rt_ref) = refs[n_act:n_act + 7]
    rest = refs[n_act + 7:]
    x1_ref, h2_ref, lg_ref = rest[-3:]
    y = None
    k0 = 0
    for a_ref in acts:
        k = a_ref.shape[-1]
        part = jnp.dot(a_ref[...].astype(BF16), w_ref[k0:k0 + k, :], preferred_element_type=F32)
        y = part if y is None else y + part
        k0 += k
    if has_bias:
        y = y + rest[0][...]
    x1 = x_ref[...] + g1_ref[...] * y
    x1_ref[...] = x1
    h2 = _norm_mod(x1, ng_ref[...], sh_ref[...], sc_ref[...])
    h_hi = h2.astype(BF16)
    h_lo = (h2 - h_hi.astype(F32)).astype(BF16)
    h2_ref[...] = h_hi
    ne = lg_ref.shape[0]
    nt = (((1,), (1,)), ((), ()))
    p_hi = lax.dot_general(rt_ref[...], h_hi, nt, preferred_element_type=F32)
    p_lo = lax.dot_general(rt_ref[0:ne, :], h_lo, nt, preferred_element_type=F32)
    lg_ref[...] = p_hi[0:ne, :] + p_hi[ne:2 * ne, :] + p_lo


def _outproj(acts, w_bf16, bias, x, mods, norm_g, router, tn=512):
    b, n, d = x.shape
    e = router.shape[1]
    tn = min(tn, n)
    row = lambda j: pl.BlockSpec((None, 1, d), lambda i, t: (i * N_MOD + j, 0, 0))
    in_specs = [pl.BlockSpec((None, tn, a.shape[-1]), lambda i, t: (i, t, 0)) for a in acts]
    in_specs += [_single(w_bf16.shape, lambda i, t: (0, 0)),
                 pl.BlockSpec((None, tn, d), lambda i, t: (i, t, 0)),
                 row(2), pl.BlockSpec((1, d), lambda i, t: (0, 0)), row(3), row(4),
                 pl.BlockSpec((2 * e, d), lambda i, t: (0, 0))]
    rt = router.T
    rt_hi = rt.astype(BF16)
    rt_lo = (rt - rt_hi.astype(F32)).astype(BF16)
    args = list(acts) + [w_bf16, x, mods, norm_g.reshape(1, d), mods, mods,
                         jnp.concatenate([rt_hi, rt_lo], axis=0)]
    if bias is not None:
        in_specs.append(pl.BlockSpec((1, d), lambda i, t: (0, 0)))
        args.append(bias.reshape(1, d))
    return pl.pallas_call(
        functools.partial(_outproj_kernel, n_act=len(acts), has_bias=bias is not None),
        grid=(b, n // tn),
        in_specs=in_specs,
        out_specs=[pl.BlockSpec((None, tn, d), lambda i, t: (i, t, 0)),
                   pl.BlockSpec((None, tn, d), lambda i, t: (i, t, 0)),
                   pl.BlockSpec((None, e, tn), lambda i, t: (i, 0, t))],
        out_shape=[jax.ShapeDtypeStruct((b, n, d), F32),
                   jax.ShapeDtypeStruct((b, n, d), BF16),
                   jax.ShapeDtypeStruct((b, e, n), F32)],
        compiler_params=_params(2),
        name="outproj_router",
    )(*args)


def _cumsum_lanes(x):
    n = x.shape[-1]
    lane = lax.broadcasted_iota(jnp.int32, x.shape, 1)
    s = 1
    while s < n:
        x = x + jnp.where(lane >= s, pltpu.roll(x, s, 1), 0)
        s *= 2
    return x


def _route_kernel(lg_ref, pos_ref, gate_ref, *, cap):
    lg = lg_ref[...]
    m = jnp.max(lg, axis=0, keepdims=True)
    ex = jnp.exp(lg - m)
    probs = ex / jnp.sum(ex, axis=0, keepdims=True)
    thr = jnp.zeros((lg.shape[0], 1), jnp.int32)
    for bit in range(30, -1, -1):
        cand = thr | (1 << bit)
        cnt = jnp.sum(jnp.where(probs >= pltpu.bitcast(cand, F32), 1.0, 0.0), axis=1, keepdims=True)
        thr = jnp.where(cnt >= cap, cand, thr)
    gt = probs >= pltpu.bitcast(thr + 1, F32)
    eq = jnp.logical_and(probs >= pltpu.bitcast(thr, F32), jnp.logical_not(gt))
    need = cap - jnp.sum(jnp.where(gt, 1.0, 0.0), axis=1, keepdims=True)
    eq_rank = _cumsum_lanes(jnp.where(eq, 1.0, 0.0))
    sel = jnp.where(gt, 1.0, jnp.where(eq, jnp.where(eq_rank <= need, 1.0, 0.0), 0.0))
    slot = _cumsum_lanes(sel) - 1.0
    pos_ref[...] = jnp.where(sel > 0.0, slot, -1.0).astype(jnp.int32)
    gate_ref[...] = jnp.where(sel > 0.0, probs, 0.0)


def _route(logits, cap):
    b, e, n = logits.shape
    spec = pl.BlockSpec((None, e, n), lambda i: (i, 0, 0))
    return pl.pallas_call(
        functools.partial(_route_kernel, cap=cap),
        grid=(b,),
        in_specs=[spec],
        out_specs=[spec, spec],
        out_shape=[jax.ShapeDtypeStruct((b, e, n), jnp.int32), jax.ShapeDtypeStruct((b, e, n), F32)],
        compiler_params=_params(1),
        name="route",
    )(logits)


MOE_ROW_CHUNK = 512


def _moe_kernel(h_ref, pos_ref, gate_ref, x_ref, g2_ref, w1_ref, w3_ref, w2_ref, *rest, cap, final):
    n_extra = 1 if final else 0
    o_ref = rest[n_extra]
    slots = (rest[n_extra + 1:n_extra + 4], rest[n_extra + 4:n_extra + 7])
    e = pl.program_id(1)
    ne = pl.num_programs(1)
    n = h_ref.shape[0]
    tr = min(MOE_ROW_CHUNK, n)

    def gather(k, slot):
        oh_s, xs_s, sg_s = slot
        pos = pos_ref[pl.ds(k, 1), :]
        gate = gate_ref[pl.ds(k, 1), :]
        hit = lax.broadcasted_iota(jnp.int32, (cap, n), 0) == pos
        onehot = jnp.where(hit, 1.0, 0.0).astype(BF16)
        oh_s[...] = onehot
        xs_s[...] = jnp.dot(onehot, h_ref[...], preferred_element_type=F32).astype(BF16)
        sg_s[...] = jnp.sum(jnp.where(hit, gate, 0.0), axis=1, keepdims=True)

    def expert(slot):
        oh_s, xs_s, sg_s = slot
        xs = xs_s[...]
        a = jnp.dot(xs, w1_ref[...], preferred_element_type=F32)
        g = jnp.dot(xs, w3_ref[...], preferred_element_type=F32)
        y = jnp.dot((_silu(a) * g).astype(BF16), w2_ref[...], preferred_element_type=F32)
        ys = (y * sg_s[...] * g2_ref[...]).astype(BF16)
        for r0 in range(0, n, tr):
            o_ref[r0:r0 + tr, :] += lax.dot_general(oh_s[:, r0:r0 + tr], ys, (((0,), (0,)), ((), ())),
                                                    preferred_element_type=F32)

    @pl.when(e == 0)
    def _():
        o_ref[...] = x_ref[...]
        gather(0, slots[0])

    nxt = jnp.minimum(e + 1, ne - 1)
    for parity in (0, 1):
        @pl.when(e % 2 == parity)
        def _(parity=parity):
            gather(nxt, slots[1 - parity])
            expert(slots[parity])

    if final:
        @pl.when(e == ne - 1)
        def _():
            def norm_rows(ci, carry):
                rows = pl.ds(pl.multiple_of(ci * tr, tr), tr)
                xf = o_ref[rows, :]
                ms = jnp.mean(xf * xf, axis=-1, keepdims=True)
                o_ref[rows, :] = xf * lax.rsqrt(ms + EPS) * rest[0][...]
                return carry
            lax.fori_loop(0, n // tr, norm_rows, 0)


def _cast_kernel(*refs):
    k = len(refs) // 2
    for x_ref, o_ref in zip(refs[:k], refs[k:]):
        o_ref[...] = x_ref[...].astype(o_ref.dtype)


def _cast_bf16(*ws):
    g = ws[0].shape[0]
    specs = [pl.BlockSpec((None,) + w.shape[1:], lambda i: (i, 0, 0)) for w in ws]
    return pl.pallas_call(
        _cast_kernel,
        grid=(g,),
        in_specs=specs,
        out_specs=specs,
        out_shape=[jax.ShapeDtypeStruct(w.shape, BF16) for w in ws],
        compiler_params=_params(1),
        name="cast_bf16",
    )(*ws)


def _moe(h2, pos, gate, x1, mods, w1, w3, w2, layer, final_g=None):
    b, n, d = x1.shape
    e = pos.shape[1]
    f = w1.shape[-1]
    cap = EC_CAPACITY * n // e
    final = final_g is not None
    in_specs = [_single((None, n, d), lambda i, k: (i, 0, 0)),
                pl.BlockSpec((None, e, n), lambda i, k: (i, 0, 0)),
                pl.BlockSpec((None, e, n), lambda i, k: (i, 0, 0)),
                _single((None, n, d), lambda i, k: (i, 0, 0)),
                pl.BlockSpec((None, 1, d), lambda i, k: (i * N_MOD + 5, 0, 0)),
                pl.BlockSpec((None, d, f), lambda i, k: (layer * e + k, 0, 0)),
                pl.BlockSpec((None, d, f), lambda i, k: (layer * e + k, 0, 0)),
                pl.BlockSpec((None, f, d), lambda i, k: (layer * e + k, 0, 0))]
    args = [h2, pos, gate, x1, mods, w1, w3, w2]
    if final:
        in_specs.append(pl.BlockSpec((1, d), lambda i, k: (0, 0)))
        args.append(final_g.reshape(1, d))
    return pl.pallas_call(
        functools.partial(_moe_kernel, cap=cap, final=final),
        grid=(b, e),
        in_specs=in_specs,
        out_specs=pl.BlockSpec((None, n, d), lambda i, k: (i, 0, 0)),
        out_shape=jax.ShapeDtypeStruct((b, n, d), F32),
        scratch_shapes=[pltpu.VMEM((cap, n), BF16), pltpu.VMEM((cap, d), BF16),
                        pltpu.VMEM((cap, 1), F32)] * 2,
        compiler_params=_params(2),
        name="moe",
    )(*args)


def _vconv_kernel(u_ref, w_ref, b_ref, lg_ref, lb_ref, o_ref, pad_s, *, n, k_taps):
    d = u_ref.shape[-1]
    halo = (k_taps - 1) // 2 * GRID_W
    tail = k_taps // 2 * GRID_W
    pad_s[0:halo, :] = jnp.zeros((halo, d), F32)
    pad_s[halo + n:halo + n + tail, :] = jnp.zeros((tail, d), F32)
    tcp = min(256, n)

    def fill(ci, carry):
        s = pl.multiple_of(ci * tcp, tcp)
        pad_s[pl.ds(halo + s, tcp), :] = u_ref[pl.ds(s, tcp), :].astype(F32)
        return carry

    lax.fori_loop(0, n // tcp, fill, 0)

    def body(r, carry):
        acc = jnp.zeros((GRID_W, d), F32)
        for k in range(k_taps):
            s = pl.multiple_of((r + k) * GRID_W, GRID_W)
            acc = acc + pad_s[pl.ds(s, GRID_W), :] * w_ref[k:k + 1, :]
        acc = acc + b_ref[...]
        mu = jnp.mean(acc, axis=-1, keepdims=True)
        xc = acc - mu
        var = jnp.mean(xc * xc, axis=-1, keepdims=True)
        y = xc * lax.rsqrt(var + EPS) * lg_ref[...] + lb_ref[...]
        o_ref[pl.ds(pl.multiple_of(r * GRID_W, GRID_W), GRID_W), :] = _silu(y).astype(o_ref.dtype)
        return carry

    lax.fori_loop(0, n // GRID_W, body, 0)


def _vconv_ln_silu(u, dw_w, dw_b, ln_g, ln_b):
    b, n, d = u.shape
    k_taps = dw_w.shape[0]
    full = lambda shape: pl.BlockSpec(shape, lambda i: tuple(0 for _ in shape))
    return pl.pallas_call(
        functools.partial(_vconv_kernel, n=n, k_taps=k_taps),
        grid=(b,),
        in_specs=[pl.BlockSpec((None, n, d), lambda i: (i, 0, 0)),
                  full((k_taps, d)), full((1, d)), full((1, d)), full((1, d))],
        out_specs=pl.BlockSpec((None, n, d), lambda i: (i, 0, 0)),
        out_shape=jax.ShapeDtypeStruct((b, n, d), BF16),
        scratch_shapes=[pltpu.VMEM((n + (k_taps - 1) * GRID_W, d), F32)],
        compiler_params=_params(1),
        name="vconv_ln_silu",
    )(u, dw_w, dw_b.reshape(1, d), ln_g.reshape(1, d), ln_b.reshape(1, d))


def kernel(x, c, ctx, c_ctx, norm1_g, norm2_g, w_mod, b_mod, w_in, conv_a_w, conv_a_b, lru_wr, lru_br, lru_wi, lru_bi, lru_lam, conv_b_w, conv_b_b, filt_w1, filt_b1, filt_w2, filt_b2, filt_w3, filt_b3, filt_freq, filt_bias, w_out, cf_w1, cf_b1, cf_dw_w, cf_dw_b, cf_ln_g, cf_ln_b, cf_w2, cf_b2, router, exp_w1, exp_w3, exp_w2, final_g):
    bsz, n, d = x.shape
    depth = w_mod.shape[0]
    assert depth == 2 and w_in.shape[0] == 1 and cf_w1.shape[0] == 1
    rw = conv_a_w.shape[-1]
    hw = filt_bias.shape[-1]
    e = router.shape[-1]
    cap = EC_CAPACITY * n // e

    rows = -(-(bsz + 1) // SUBLANES) * SUBLANES
    cvec = jnp.concatenate([c, c_ctx[None], jnp.zeros((rows - bsz - 1, d), F32)], axis=0)
    mods_all = _adaln(cvec, w_mod, b_mod)
    mods = [mods_all[l].reshape(rows * N_MOD, 1, d) for l in range(depth)]

    w_in0 = w_in[0].astype(BF16)
    wg, bg = _lru_gate_weights(lru_wr[0], lru_wi[0], lru_br[0], lru_bi[0])
    ctx_xa = _norm_matmul(ctx, norm1_g[0], mods[0], lambda i: bsz, 0, 1, w_in0[:, rw:2 * rw],
                          out_dtype=BF16)
    h0_ctx = jnp.zeros((bsz, 2, rw), F32)
    _, ctx_state = _rglru(ctx_xa, 0, None, 0, conv_a_w[0], conv_a_b[0], wg, bg, lru_lam[0], h0_ctx)
    proj = _norm_matmul(x, norm1_g[0], mods[0], lambda i: i, 0, 1, w_in0, out_dtype=BF16)
    y_a, _ = _rglru(proj, 1, proj, 0, conv_a_w[0], conv_a_b[0], wg, bg, lru_lam[0], ctx_state,
                    out_dtype=BF16)

    fsum, fdif, kn = _hyena_filters(n, filt_w1[0], filt_b1[0], filt_w2[0], filt_b2[0],
                                    filt_w3[0], filt_b3[0], filt_freq[0])
    cm, sm = (jnp.asarray(t) for t in _dft_tables(n))
    kr, ki = _hyena_spectrum(cm, sm, fsum, fdif)
    z = _hyena(proj, 2 * rw, conv_b_w[0], conv_b_b[0], cm.astype(BF16), sm.astype(BF16),
               kr, ki, kn, filt_bias[0], out_dtype=BF16)

    ff = exp_w1.shape[-1]
    ew1, ew3, ew2 = _cast_bf16(exp_w1.reshape(depth * e, d, ff), exp_w3.reshape(depth * e, d, ff),
                               exp_w2.reshape(depth * e, ff, d))

    x1, h2, logits = _outproj([y_a, z], w_out[0].astype(BF16), None, x, mods[0], norm2_g[0], router[0])
    pos, gate = _route(logits, cap)
    xl = _moe(h2, pos, gate, x1, mods[0], ew1, ew3, ew2, 0)

    u = _norm_matmul(xl, norm1_g[1], mods[1], lambda i: i, 0, 1, cf_w1[0].astype(BF16),
                     bias=cf_b1[0], glu=True, out_dtype=BF16)
    v = _vconv_ln_silu(u, cf_dw_w[0], cf_dw_b[0], cf_ln_g[0], cf_ln_b[0])
    x1, h2, logits = _outproj([v], cf_w2[0].astype(BF16), cf_b2[0], xl, mods[1], norm2_g[1], router[1])
    pos, gate = _route(logits, cap)
    return _moe(h2, pos, gate, x1, mods[1], ew1, ew3, ew2, 1, final_g=final_g)
```

```python
import functools
import math

import numpy as np
import jax
import jax.numpy as jnp
from jax import lax
from jax.experimental import pallas as pl
from jax.experimental.pallas import tpu as pltpu

F32 = jnp.float32
BF16 = jnp.bfloat16
HIGHEST = lax.Precision.HIGHEST

EPS = 1e-6
GRID_W = 64
N_MOD = 6
LRU_HEADS = 8
LRU_C = 8.0
HYENA_ORDER = 2
HYENA_DECAY_TARGET = 1e-2
HYENA_FAST_PCT = 0.3
HYENA_SLOW_PCT = 1.5
HYENA_SHIFT = 0.05
EC_CAPACITY = 2

V7X_VMEM_BYTES = 64 * 1024 * 1024
VMEM_LIMIT = V7X_VMEM_BYTES - 4 * 1024 * 1024
SUBLANES = 8


def _params(n_axes):
    return pltpu.CompilerParams(dimension_semantics=("arbitrary",) * n_axes,
                                vmem_limit_bytes=VMEM_LIMIT)


def _single(block_shape, index_map):
    return pl.BlockSpec(block_shape, index_map, pipeline_mode=pl.Buffered(1))


def _silu(x):
    return x * jax.nn.sigmoid(x)


def _split_hi_lo(x):
    hi = x.astype(BF16)
    lo = (x - hi.astype(F32)).astype(BF16)
    return jnp.concatenate([hi, lo], axis=-1)


def _norm_mod(xf, g, shift, scale):
    ms = jnp.mean(xf * xf, axis=-1, keepdims=True)
    y = xf * lax.rsqrt(ms + EPS) * g
    return y * (1.0 + scale) + shift


def _adaln_kernel(c_ref, w_ref, b_ref, o_ref):
    s = _silu(c_ref[...])
    o_ref[...] = jnp.dot(s, w_ref[...], precision=HIGHEST, preferred_element_type=F32) + b_ref[...]


def _adaln(cvec, w_mod, b_mod, tc=512):
    depth, d, dm = w_mod.shape
    r = cvec.shape[0]
    return pl.pallas_call(
        _adaln_kernel,
        grid=(depth, dm // tc),
        in_specs=[pl.BlockSpec((r, d), lambda l, j: (0, 0)),
                  pl.BlockSpec((None, d, tc), lambda l, j: (l, 0, j)),
                  pl.BlockSpec((None, 1, tc), lambda l, j: (l, 0, j))],
        out_specs=pl.BlockSpec((None, r, tc), lambda l, j: (l, 0, j)),
        out_shape=jax.ShapeDtypeStruct((depth, r, dm), F32),
        compiler_params=_params(2),
        name="adaln",
    )(cvec, w_mod, b_mod.reshape(depth, 1, dm))


def _norm_matmul_kernel(x_ref, g_ref, sh_ref, sc_ref, w_ref, *rest, has_bias, glu):
    o_ref = rest[-1]
    h = _norm_mod(x_ref[...], g_ref[...], sh_ref[...], sc_ref[...])
    y = jnp.dot(h.astype(BF16), w_ref[...], preferred_element_type=F32)
    if has_bias:
        y = y + rest[0][...]
    if glu:
        half = y.shape[-1] // 2
        y = y[:, :half] * jax.nn.sigmoid(y[:, half:])
    o_ref[...] = y.astype(o_ref.dtype)


def _norm_matmul(x, g, mods, mod_row, shift_j, scale_j, w_bf16, bias=None, glu=False, tn=512,
                 out_dtype=F32):
    b, n, d = x.shape
    nout = w_bf16.shape[1]
    tn = min(tn, n)
    out_w = nout // 2 if glu else nout
    in_specs = [pl.BlockSpec((None, tn, d), lambda i, j: (i, j, 0)),
                pl.BlockSpec((1, d), lambda i, j: (0, 0)),
                pl.BlockSpec((None, 1, d), lambda i, j: (mod_row(i) * N_MOD + shift_j, 0, 0)),
                pl.BlockSpec((None, 1, d), lambda i, j: (mod_row(i) * N_MOD + scale_j, 0, 0)),
                _single((d, nout), lambda i, j: (0, 0))]
    args = [x, g.reshape(1, d), mods, mods, w_bf16]
    if bias is not None:
        in_specs.append(pl.BlockSpec((1, nout), lambda i, j: (0, 0)))
        args.append(bias.reshape(1, nout))
    return pl.pallas_call(
        functools.partial(_norm_matmul_kernel, has_bias=bias is not None, glu=glu),
        grid=(b, n // tn),
        in_specs=in_specs,
        out_specs=pl.BlockSpec((None, tn, out_w), lambda i, j: (i, j, 0)),
        out_shape=jax.ShapeDtypeStruct((b, n, out_w), out_dtype),
        compiler_params=_params(2),
        name="norm_matmul",
    )(*args)


HALO = 16


def _shift_rows(cur, k, prev, nxt, has_prev, has_next):
    t = cur.shape[0]
    row = lax.broadcasted_iota(jnp.int32, cur.shape, 0)
    if k < 0:
        out = pltpu.roll(cur, -k, 0)
        for r in range(-k):
            src = prev[HALO + k + r:HALO + k + r + 1, :]
            fill = jnp.where(has_prev, src, 0.0)
            out = jnp.where(row == r, fill, out)
    else:
        out = pltpu.roll(cur, t - k, 0)
        for r in range(k):
            src = nxt[r:r + 1, :]
            fill = jnp.where(has_next, src, 0.0)
            out = jnp.where(row == t - k + r, fill, out)
    return out


def _dwconv_chunk(x_ref, s, t, n, w_ref, b_ref, pad_left):
    cur = x_ref[pl.ds(s, t), :].astype(F32)
    prev = x_ref[pl.ds(pl.multiple_of(jnp.maximum(s - HALO, 0), HALO), HALO), :].astype(F32)
    nxt = x_ref[pl.ds(pl.multiple_of(jnp.minimum(s + t, n - HALO), HALO), HALO), :].astype(F32)
    has_prev = s > 0
    has_next = s + t < n
    k_taps = w_ref.shape[0]
    acc = None
    for k in range(k_taps):
        off = k - pad_left
        xs = cur if off == 0 else _shift_rows(cur, off, prev, nxt, has_prev, has_next)
        term = xs * w_ref[k:k + 1, :]
        acc = term if acc is None else acc + term
    return acc + b_ref[...]


GATE_CHUNK = 256
SCAN_CHUNK = 64


def _scan_rows(a, b, reverse):
    t = a.shape[0]
    row = lax.broadcasted_iota(jnp.int32, a.shape, 0)
    s = 1
    while s < t:
        if reverse:
            valid = row < t - s
            sh = t - s
        else:
            valid = row >= s
            sh = s
        a_s = jnp.where(valid, pltpu.roll(a, sh, 0), 1.0)
        b_s = jnp.where(valid, pltpu.roll(b, sh, 0), 0.0)
        b = a * b_s + b
        a = a * a_s
        s *= 2
    return a, b


def _gelu_tanh(x):
    return 0.5 * x * (1.0 + jnp.tanh(math.sqrt(2.0 / math.pi) * (x + 0.044715 * (x * x * x))))


def _rglru_kernel(*refs, n, with_gate):
    if with_gate:
        (xa_ref, gate_ref, cw_ref, cb_ref, wg_ref, bg_ref, lam_ref, h0_ref,
         y_ref, hl_ref, af_s, uf_s, ab_s, ub_s) = refs
    else:
        (xa_ref, cw_ref, cb_ref, wg_ref, bg_ref, lam_ref, h0_ref,
         hl_ref, af_s, uf_s, ab_s, ub_s) = refs
    w = xa_ref.shape[-1]
    tg = min(GATE_CHUNK, n)
    ts = min(SCAN_CHUNK, n)
    lam = lam_ref[...]
    sp = jnp.maximum(-lam, 0.0) + jnp.log(1.0 + jnp.exp(-jnp.abs(lam)))

    def gates(ci, carry):
        s = pl.multiple_of(ci * tg, tg)
        xa = _dwconv_chunk(xa_ref, s, tg, n, cw_ref, cb_ref, 1)
        z = jnp.dot(xa.astype(BF16), wg_ref[...], preferred_element_type=F32) + bg_ref[...]
        for d, (a_s, u_s) in enumerate(((af_s, uf_s), (ab_s, ub_s))):
            r = 0.5 * jnp.tanh(0.5 * z[:, (2 * d) * w:(2 * d + 1) * w]) + 0.5
            i = 0.5 * jnp.tanh(0.5 * z[:, (2 * d + 1) * w:(2 * d + 2) * w]) + 0.5
            log_a = (-LRU_C) * r * sp[d:d + 1, :]
            a_s[pl.ds(s, tg), :] = jnp.exp(log_a)
            u_s[pl.ds(s, tg), :] = jnp.sqrt(1.0 - jnp.exp(2.0 * log_a)) * i * xa
        return carry

    lax.fori_loop(0, n // tg, gates, 0)

    nc = n // ts

    def fwd(ci, carry):
        s = pl.multiple_of(ci * ts, ts)
        a_c, h_l = _scan_rows(af_s[pl.ds(s, ts), :], uf_s[pl.ds(s, ts), :], False)
        h = h_l + a_c * carry
        uf_s[pl.ds(s, ts), :] = h
        return h[ts - 1:ts, :]

    hf_last = lax.fori_loop(0, nc, fwd, h0_ref[0:1, :])

    def bwd(ci, carry):
        s = pl.multiple_of((nc - 1 - ci) * ts, ts)
        a_c, h_l = _scan_rows(ab_s[pl.ds(s, ts), :], ub_s[pl.ds(s, ts), :], True)
        h = h_l + a_c * carry
        if with_gate:
            hsum = uf_s[pl.ds(s, ts), :] + h
            gate = gate_ref[pl.ds(s, ts), :].astype(F32)
            y_ref[pl.ds(s, ts), :] = (_gelu_tanh(gate) * hsum).astype(y_ref.dtype)
        return h[0:1, :]

    hb_last = lax.fori_loop(0, nc, bwd, h0_ref[1:2, :])
    hl_ref[0:1, :] = hf_last
    hl_ref[1:2, :] = hb_last


def _rglru(xa_src, xa_col, gate_src, gate_col, conv_w, conv_b, wg_bf16, bg, lam, h0, out_dtype=F32):
    b, n, _ = xa_src.shape
    w = conv_w.shape[1]
    with_gate = gate_src is not None
    seq = lambda col: pl.BlockSpec((None, n, w), lambda i: (i, 0, col))
    full = lambda shape: pl.BlockSpec(shape, lambda i: tuple(0 for _ in shape))
    in_specs = [seq(xa_col)]
    args = [xa_src]
    if with_gate:
        in_specs.append(seq(gate_col))
        args.append(gate_src)
    in_specs += [full(conv_w.shape), full((1, w)), _single(wg_bf16.shape, lambda i: (0, 0)),
                 full((1, 4 * w)), full((2, w)), pl.BlockSpec((None, 2, w), lambda i: (i, 0, 0))]
    args += [conv_w, conv_b.reshape(1, w), wg_bf16, bg.reshape(1, 4 * w), lam, h0]
    out_specs = [pl.BlockSpec((None, 2, w), lambda i: (i, 0, 0))]
    out_shape = [jax.ShapeDtypeStruct((b, 2, w), F32)]
    if with_gate:
        out_specs.insert(0, pl.BlockSpec((None, n, w), lambda i: (i, 0, 0)))
        out_shape.insert(0, jax.ShapeDtypeStruct((b, n, w), out_dtype))
    outs = pl.pallas_call(
        functools.partial(_rglru_kernel, n=n, with_gate=with_gate),
        grid=(b,),
        in_specs=in_specs,
        out_specs=out_specs,
        out_shape=out_shape,
        scratch_shapes=[pltpu.VMEM((n, w), F32)] * 4,
        compiler_params=_params(1),
        name="rglru_gated" if with_gate else "rglru_state",
    )(*args)
    return (outs[0], outs[1]) if with_gate else (None, outs[0])


def _lru_gate_weights(lru_wr, lru_wi, lru_br, lru_bi):
    def dense(wh):
        heads, hd, _ = wh.shape
        eye = jnp.eye(heads, dtype=wh.dtype)
        return jnp.einsum('hij,hg->higj', wh, eye).reshape(heads * hd, heads * hd)
    wg = jnp.concatenate([dense(lru_wr[0]), dense(lru_wi[0]), dense(lru_wr[1]), dense(lru_wi[1])], axis=1)
    bg = jnp.concatenate([lru_br[0], lru_bi[0], lru_br[1], lru_bi[1]], axis=0)
    return wg.astype(BF16), bg


@functools.lru_cache(maxsize=None)
def _dft_tables(n):
    m = 2 * n
    k = (np.arange(n, dtype=np.int64)[:, None] * np.arange(n, dtype=np.int64)[None, :]) % m
    ang = 2.0 * np.pi * np.arange(m, dtype=np.float64) / m
    cm = np.cos(ang)[k].astype(np.float32)
    sm = np.sin(ang)[k].astype(np.float32)
    return cm, sm


@functools.lru_cache(maxsize=None)
def _filter_features(n, emb):
    t = np.linspace(0.0, 1.0, n, dtype=np.float32)[:, None]
    bands = (emb - 1) // 2
    w = (np.float32(2.0 * math.pi / n) * np.arange(n, dtype=np.float32))[:, None]
    f = np.linspace(1e-4, bands - 1, bands, dtype=np.float32)[None, :]
    z = np.concatenate([t, np.cos(f * w), -np.sin(f * w)], axis=-1).astype(np.float32)
    return z


@functools.lru_cache(maxsize=None)
def _hyena_deltas(width):
    d = np.abs(np.linspace(math.log(HYENA_DECAY_TARGET) / HYENA_SLOW_PCT,
                           math.log(HYENA_DECAY_TARGET) / HYENA_FAST_PCT, width, dtype=np.float32))
    return d.reshape(1, width).astype(np.float32)


def _filter_kernel(z_ref, t_ref, w1_ref, b1_ref, w2_ref, b2_ref, w3_ref, b3_ref, fr_ref, dl_ref,
                   fsum_ref, fdif_ref, kn_ref, *, n, tn):
    j = pl.program_id(0)
    hw = dl_ref.shape[-1]
    hid = jnp.sin(fr_ref[0:1, :] * (jnp.dot(z_ref[...], w1_ref[...], precision=HIGHEST,
                                            preferred_element_type=F32) + b1_ref[...]))
    hid = jnp.sin(fr_ref[1:2, :] * (jnp.dot(hid, w2_ref[...], precision=HIGHEST,
                                            preferred_element_type=F32) + b2_ref[...]))
    filt = jnp.dot(hid, w3_ref[...], precision=HIGHEST, preferred_element_type=F32) + b3_ref[...]
    window = jnp.exp(-t_ref[...] * dl_ref[...]) + HYENA_SHIFT
    row = lax.broadcasted_iota(jnp.int32, (tn, hw), 0) + j * tn
    alt = jnp.where((row & 1) == 0, 1.0, -1.0)

    @pl.when(j == 0)
    def _():
        kn_ref[...] = jnp.zeros_like(kn_ref)

    for o in range(HYENA_ORDER):
        fwd = filt[:, (2 * o) * hw:(2 * o + 1) * hw] * window
        bwd = filt[:, (2 * o + 1) * hw:(2 * o + 2) * hw] * window
        bwd = jnp.where(row == 0, 0.0, bwd)
        fs = fwd + bwd
        fsum_ref[o] = _split_hi_lo(fs)
        fdif_ref[o] = _split_hi_lo(bwd - fwd)
        kn_ref[o] += jnp.sum(fs * alt, axis=0, keepdims=True) * (1.0 / (2 * n))


def _hyena_filters(n, filt_w1, filt_b1, filt_w2, filt_b2, filt_w3, filt_b3, filt_freq, tn=256):
    emb, hidden = filt_w1.shape
    hw = filt_w3.shape[1] // (2 * HYENA_ORDER)
    tn = min(tn, n)
    z = jnp.asarray(_filter_features(n, emb))
    t = jnp.asarray(np.linspace(0.0, 1.0, n, dtype=np.float32)[:, None])
    dl = jnp.asarray(_hyena_deltas(hw))
    full = lambda shape: pl.BlockSpec(shape, lambda j: tuple(0 for _ in shape))
    return pl.pallas_call(
        functools.partial(_filter_kernel, n=n, tn=tn),
        grid=(n // tn,),
        in_specs=[pl.BlockSpec((tn, emb), lambda j: (j, 0)),
                  pl.BlockSpec((tn, 1), lambda j: (j, 0)),
                  full((emb, hidden)), full((1, hidden)), full((hidden, hidden)), full((1, hidden)),
                  full((hidden, 2 * HYENA_ORDER * hw)), full((1, 2 * HYENA_ORDER * hw)),
                  full((2, hidden)), full((1, hw))],
        out_specs=[pl.BlockSpec((HYENA_ORDER, tn, 2 * hw), lambda j: (0, j, 0)),
                   pl.BlockSpec((HYENA_ORDER, tn, 2 * hw), lambda j: (0, j, 0)),
                   pl.BlockSpec((HYENA_ORDER, 1, hw), lambda j: (0, 0, 0))],
        out_shape=[jax.ShapeDtypeStruct((HYENA_ORDER, n, 2 * hw), BF16),
                   jax.ShapeDtypeStruct((HYENA_ORDER, n, 2 * hw), BF16),
                   jax.ShapeDtypeStruct((HYENA_ORDER, 1, hw), F32)],
        compiler_params=_params(1),
        name="hyena_filter",
    )(z, t, filt_w1, filt_b1.reshape(1, -1), filt_w2, filt_b2.reshape(1, -1),
      filt_w3, filt_b3.reshape(1, -1), filt_freq, dl)


def _dot_hi_lo(t_hi, t_lo, f_cat):
    hw = f_cat.shape[-1] // 2
    p = jnp.dot(t_hi, f_cat, preferred_element_type=F32)
    return p[:, :hw] + p[:, hw:] + jnp.dot(t_lo, f_cat[:, :hw], preferred_element_type=F32)


def _spectrum_kernel(cmh_ref, cml_ref, smh_ref, sml_ref, fsum_ref, fdif_ref, kr_ref, ki_ref, *, n, tm):
    j = pl.program_id(1)
    row = lax.broadcasted_iota(jnp.int32, kr_ref.shape, 0) + j * tm
    scale = jnp.where(row == 0, 1.0 / (2 * n), 2.0 / (2 * n))
    kr_ref[...] = scale * _dot_hi_lo(cmh_ref[...], cml_ref[...], fsum_ref[...])
    ki_ref[...] = scale * _dot_hi_lo(smh_ref[...], sml_ref[...], fdif_ref[...])


def _hyena_spectrum(cm_hi, cm_lo, sm_hi, sm_lo, fsum, fdif, tm=512):
    order, n, hw2 = fsum.shape
    hw = hw2 // 2
    tm = min(tm, n)
    tab = pl.BlockSpec((tm, n), lambda o, j: (j, 0))
    filt = pl.BlockSpec((None, n, hw2), lambda o, j: (o, 0, 0))
    out = pl.BlockSpec((None, tm, hw), lambda o, j: (o, j, 0))
    return pl.pallas_call(
        functools.partial(_spectrum_kernel, n=n, tm=tm),
        grid=(order, n // tm),
        in_specs=[tab, tab, tab, tab, filt, filt],
        out_specs=[out, out],
        out_shape=[jax.ShapeDtypeStruct((order, n, hw), F32)] * 2,
        compiler_params=_params(2),
        name="hyena_spectrum",
    )(cm_hi, cm_lo, sm_hi, sm_lo, fsum, fdif)


HYENA_CHUNK = 512


def _hyena_kernel(v_ref, g0_ref, g1_ref, cwv_ref, cw0_ref, cw1_ref, cbv_ref, cb0_ref, cb1_ref,
                  cm_ref, sm_ref, kr_ref, ki_ref, kn_ref, fb_ref, z_ref,
                  u_s, ub_s, zc_s, zs_s, *, n):
    ct = v_ref.shape[-1]
    tc = min(HYENA_CHUNK, n)
    nchunks = n // tc
    zero_row = jnp.zeros((1, ct), F32)

    def alt_sign(s):
        row = lax.broadcasted_iota(jnp.int32, (tc, ct), 0) + s
        return jnp.where((row & 1) == 0, 1.0, -1.0)

    def put_u(s, u, acc):
        u_s[pl.ds(s, tc), :] = u
        ub_s[pl.ds(s, tc), :] = u.astype(BF16)
        return acc + jnp.sum(u * alt_sign(s), axis=0, keepdims=True)

    def first(ci, acc):
        s = pl.multiple_of(ci * tc, tc)
        return put_u(s, _dwconv_chunk(v_ref, s, tc, n, cwv_ref, cbv_ref, 1), acc)

    nyq_sum = lax.fori_loop(0, nchunks, first, zero_row)

    for o, (g_ref, cw_ref, cb_ref) in enumerate(((g0_ref, cw0_ref, cb0_ref), (g1_ref, cw1_ref, cb1_ref))):
        last = o == HYENA_ORDER - 1

        def spectrum(ci, carry, o=o):
            rows = pl.ds(pl.multiple_of(ci * tc, tc), tc)
            xre = jnp.dot(cm_ref[rows, :], ub_s[...], preferred_element_type=F32)
            xim = jnp.dot(sm_ref[rows, :], ub_s[...], preferred_element_type=F32)
            kr = kr_ref[o, rows, :]
            ki = ki_ref[o, rows, :]
            zc_s[rows, :] = (xre * kr + xim * ki).astype(BF16)
            zs_s[rows, :] = (xim * kr - xre * ki).astype(BF16)
            return carry

        lax.fori_loop(0, nchunks, spectrum, 0)
        nyq = nyq_sum * kn_ref[o]

        def inverse(ci, acc, o=o, g_ref=g_ref, cw_ref=cw_ref, cb_ref=cb_ref, last=last, nyq=nyq):
            s = pl.multiple_of(ci * tc, tc)
            rows = pl.ds(s, tc)
            y = (jnp.dot(cm_ref[rows, :], zc_s[...], preferred_element_type=F32)
                 + jnp.dot(sm_ref[rows, :], zs_s[...], preferred_element_type=F32))
            y = y + alt_sign(s) * nyq + u_s[rows, :] * fb_ref[o]
            z = _dwconv_chunk(g_ref, s, tc, n, cw_ref, cb_ref, 1) * y
            if last:
                z_ref[rows, :] = z.astype(z_ref.dtype)
                return acc
            return put_u(s, z, acc)

        nyq_sum = lax.fori_loop(0, nchunks, inverse, zero_row)


def _hyena(proj, hy_col0, conv_w, conv_b, cm_bf16, sm_bf16, kr, ki, kn, filt_bias, ct=256,
           out_dtype=F32):
    b, n, _ = proj.shape
    hw = kr.shape[-1]
    ct = min(ct, hw)
    nct = hw // ct
    base = hy_col0 // ct
    k_taps = conv_w.shape[0]

    def seq(j):
        return pl.BlockSpec((None, n, ct), lambda c, i: (i, 0, base + j * nct + c))

    def cw(j):
        return pl.BlockSpec((k_taps, ct), lambda c, i: (0, j * nct + c))

    def cb(j):
        return pl.BlockSpec((1, ct), lambda c, i: (0, j * nct + c))

    kspec = _single((HYENA_ORDER, n, ct), lambda c, i: (0, 0, c))
    rowspec = pl.BlockSpec((HYENA_ORDER, 1, ct), lambda c, i: (0, 0, c))
    conv_b2 = conv_b.reshape(1, -1)
    return pl.pallas_call(
        functools.partial(_hyena_kernel, n=n),
        grid=(nct, b),
        in_specs=[seq(0), seq(1), seq(2), cw(0), cw(1), cw(2), cb(0), cb(1), cb(2),
                  _single((n, n), lambda c, i: (0, 0)), _single((n, n), lambda c, i: (0, 0)),
                  kspec, kspec, rowspec, rowspec],
        out_specs=pl.BlockSpec((None, n, ct), lambda c, i: (i, 0, c)),
        out_shape=jax.ShapeDtypeStruct((b, n, hw), out_dtype),
        scratch_shapes=[pltpu.VMEM((n, ct), F32), pltpu.VMEM((n, ct), BF16),
                        pltpu.VMEM((n, ct), BF16), pltpu.VMEM((n, ct), BF16)],
        compiler_params=_params(2),
        name="hyena",
    )(proj, proj, proj, conv_w, conv_w, conv_w, conv_b2, conv_b2, conv_b2,
      cm_bf16, sm_bf16, kr, ki, kn, filt_bias.reshape(HYENA_ORDER, 1, hw))


def _outproj_kernel(*refs, n_act, has_bias):
    acts = refs[:n_act]
    (w_ref, x_ref, g1_ref, ng_ref, sh_ref, sc_ref, rt_ref) = refs[n_act:n_act + 7]
    rest = refs[n_act + 7:]
    x1_ref, h2_ref, lg_ref = rest[-3:]
    y = None
    k0 = 0
    for a_ref in acts:
        k = a_ref.shape[-1]
        part = jnp.dot(a_ref[...].astype(BF16), w_ref[k0:k0 + k, :], preferred_element_type=F32)
        y = part if y is None else y + part
        k0 += k
    if has_bias:
        y = y + rest[0][...]
    x1 = x_ref[...] + g1_ref[...] * y
    x1_ref[...] = x1
    h2 = _norm_mod(x1, ng_ref[...], sh_ref[...], sc_ref[...])
    h_hi = h2.astype(BF16)
    h_lo = (h2 - h_hi.astype(F32)).astype(BF16)
    h2_ref[...] = h2
    ne = lg_ref.shape[0]
    nt = (((1,), (1,)), ((), ()))
    p_hi = lax.dot_general(rt_ref[...], h_hi, nt, preferred_element_type=F32)
    p_lo = lax.dot_general(rt_ref[0:ne, :], h_lo, nt, preferred_element_type=F32)
    lg_ref[...] = p_hi[0:ne, :] + p_hi[ne:2 * ne, :] + p_lo


def _outproj(acts, w_bf16, bias, x, mods, norm_g, router, tn=512):
    b, n, d = x.shape
    e = router.shape[1]
    tn = min(tn, n)
    row = lambda j: pl.BlockSpec((None, 1, d), lambda i, t: (i * N_MOD + j, 0, 0))
    in_specs = [pl.BlockSpec((None, tn, a.shape[-1]), lambda i, t: (i, t, 0)) for a in acts]
    in_specs += [_single(w_bf16.shape, lambda i, t: (0, 0)),
                 pl.BlockSpec((None, tn, d), lambda i, t: (i, t, 0)),
                 row(2), pl.BlockSpec((1, d), lambda i, t: (0, 0)), row(3), row(4),
                 pl.BlockSpec((2 * e, d), lambda i, t: (0, 0))]
    rt = router.T
    rt_hi = rt.astype(BF16)
    rt_lo = (rt - rt_hi.astype(F32)).astype(BF16)
    args = list(acts) + [w_bf16, x, mods, norm_g.reshape(1, d), mods, mods,
                         jnp.concatenate([rt_hi, rt_lo], axis=0)]
    if bias is not None:
        in_specs.append(pl.BlockSpec((1, d), lambda i, t: (0, 0)))
        args.append(bias.reshape(1, d))
    return pl.pallas_call(
        functools.partial(_outproj_kernel, n_act=len(acts), has_bias=bias is not None),
        grid=(b, n // tn),
        in_specs=in_specs,
        out_specs=[pl.BlockSpec((None, tn, d), lambda i, t: (i, t, 0)),
                   pl.BlockSpec((None, tn, d), lambda i, t: (i, t, 0)),
                   pl.BlockSpec((None, e, tn), lambda i, t: (i, 0, t))],
        out_shape=[jax.ShapeDtypeStruct((b, n, d), F32),
                   jax.ShapeDtypeStruct((b, n, d), F32),
                   jax.ShapeDtypeStruct((b, e, n), F32)],
        compiler_params=_params(2),
        name="outproj_router",
    )(*args)


def _cumsum_lanes(x):
    n = x.shape[-1]
    lane = lax.broadcasted_iota(jnp.int32, x.shape, 1)
    s = 1
    while s < n:
        x = x + jnp.where(lane >= s, pltpu.roll(x, s, 1), 0)
        s *= 2
    return x


def _compact_lanes(slot, sel):
    n = slot.shape[-1]
    lane = lax.broadcasted_iota(jnp.int32, slot.shape, 1)
    val = lane
    enc = jnp.where(sel, ((lane - slot) << 1) | 1, 0)
    j = 0
    while (1 << j) < n:
        sh = 1 << j
        inside = lane < n - sh
        src_val = pltpu.roll(val, n - sh, 1)
        src_enc = jnp.where(inside, pltpu.roll(enc, n - sh, 1), 0)
        take = ((src_enc >> (j + 1)) & 1) > 0
        stay = (enc & (1 - ((enc >> (j + 1)) & 1))) > 0
        val = jnp.where(take, src_val, val)
        enc = jnp.where(take, src_enc, jnp.where(stay, enc, 0))
        j += 1
    return val


def _route_kernel(lg_ref, pos_ref, gate_ref, idx_ref, *, cap):
    for s in range(lg_ref.shape[0]):
        _route_one(lg_ref.at[s], pos_ref.at[s], gate_ref.at[s], idx_ref.at[s], cap)


def _route_one(lg_ref, pos_ref, gate_ref, idx_ref, cap):
    lg = lg_ref[...]
    m = jnp.max(lg, axis=0, keepdims=True)
    ex = jnp.exp(lg - m)
    probs = ex / jnp.sum(ex, axis=0, keepdims=True)
    thr = jnp.zeros((lg.shape[0], 1), jnp.int32)
    for bit in range(30, -1, -1):
        cand = thr | (1 << bit)
        cnt = jnp.sum(jnp.where(probs >= pltpu.bitcast(cand, F32), 1.0, 0.0), axis=1, keepdims=True)
        thr = jnp.where(cnt >= cap, cand, thr)
    gt = probs >= pltpu.bitcast(thr + 1, F32)
    eq = jnp.logical_and(probs >= pltpu.bitcast(thr, F32), jnp.logical_not(gt))
    need = cap - jnp.sum(jnp.where(gt, 1.0, 0.0), axis=1, keepdims=True)
    eq_rank = _cumsum_lanes(jnp.where(eq, 1.0, 0.0))
    sel = jnp.where(gt, 1.0, jnp.where(eq, jnp.where(eq_rank <= need, 1.0, 0.0), 0.0))
    slot = (_cumsum_lanes(sel) - 1.0).astype(jnp.int32)
    chosen = sel > 0.0
    pos_ref[...] = jnp.where(chosen, slot, -1)
    gate_ref[...] = jnp.where(chosen, probs, 0.0)
    idx_ref[...] = _compact_lanes(slot, chosen)[:, :idx_ref.shape[-1]]


ROUTE_SAMPLES = 4


def _route(logits, cap):
    b, e, n = logits.shape
    g = math.gcd(ROUTE_SAMPLES, b)
    capw = -(-cap // 128) * 128
    spec = pl.BlockSpec((g, e, n), lambda i: (i, 0, 0))
    return pl.pallas_call(
        functools.partial(_route_kernel, cap=cap),
        grid=(b // g,),
        in_specs=[spec],
        out_specs=[spec, spec, pl.BlockSpec((g, e, capw), lambda i: (i, 0, 0))],
        out_shape=[jax.ShapeDtypeStruct((b, e, n), jnp.int32), jax.ShapeDtypeStruct((b, e, n), F32),
                   jax.ShapeDtypeStruct((b, e, capw), jnp.int32)],
        compiler_params=_params(1),
        name="route",
    )(logits)


MOE_ROW_CHUNK = 512


def _moe_kernel(idx_ref, h_ref, pos_ref, gate_ref, x_ref, g2_ref, w1_ref, w3_ref, w2_ref, *rest,
                cap, final):
    n_extra = 1 if final else 0
    o_ref, xs_s = rest[n_extra:n_extra + 2]
    e = pl.program_id(1)
    ne = pl.num_programs(1)
    n = h_ref.shape[0]
    tr = min(MOE_ROW_CHUNK, n)

    def gather_rows(k):
        for c in range(cap):
            xs_s[c:c + 1, :] = h_ref[pl.ds(idx_ref[k, c], 1), :]

    @pl.when(e == 0)
    def _():
        o_ref[...] = x_ref[...]
        gather_rows(0)

    xs = xs_s[...].astype(BF16)
    gather_rows(jnp.minimum(e + 1, ne - 1))
    pos = pos_ref[pl.ds(e, 1), :]
    gate = gate_ref[pl.ds(e, 1), :]
    hit = lax.broadcasted_iota(jnp.int32, (cap, n), 0) == pos
    onehot = jnp.where(hit, 1.0, 0.0).astype(BF16)
    a = jnp.dot(xs, w1_ref[...], preferred_element_type=F32)
    g = jnp.dot(xs, w3_ref[...], preferred_element_type=F32)
    y = jnp.dot((_silu(a) * g).astype(BF16), w2_ref[...], preferred_element_type=F32)
    slot_gate = jnp.sum(jnp.where(hit, gate, 0.0), axis=1, keepdims=True)
    ys = (y * slot_gate * g2_ref[...]).astype(BF16)
    for r0 in range(0, n, tr):
        o_ref[r0:r0 + tr, :] += lax.dot_general(onehot[:, r0:r0 + tr], ys, (((0,), (0,)), ((), ())),
                                                preferred_element_type=F32)

    if final:
        @pl.when(e == ne - 1)
        def _():
            def norm_rows(ci, carry):
                rows = pl.ds(pl.multiple_of(ci * tr, tr), tr)
                xf = o_ref[rows, :]
                ms = jnp.mean(xf * xf, axis=-1, keepdims=True)
                o_ref[rows, :] = xf * lax.rsqrt(ms + EPS) * rest[0][...]
                return carry
            lax.fori_loop(0, n // tr, norm_rows, 0)


def _cast_kernel(*refs):
    k = len(refs) // 2
    for x_ref, o_ref in zip(refs[:k], refs[k:]):
        o_ref[...] = x_ref[...].astype(o_ref.dtype)


def _cast_bf16(*ws):
    g = ws[0].shape[0]
    specs = [pl.BlockSpec((None,) + w.shape[1:], lambda i: (i, 0, 0)) for w in ws]
    return pl.pallas_call(
        _cast_kernel,
        grid=(g,),
        in_specs=specs,
        out_specs=specs,
        out_shape=[jax.ShapeDtypeStruct(w.shape, BF16) for w in ws],
        compiler_params=_params(1),
        name="cast_bf16",
    )(*ws)


def _moe(h2, pos, gate, idx, x1, mods, w1, w3, w2, layer, final_g=None):
    b, n, d = x1.shape
    e = pos.shape[1]
    f = w1.shape[-1]
    cap = EC_CAPACITY * n // e
    final = final_g is not None
    in_specs = [pl.BlockSpec((None, e, idx.shape[-1]), lambda i, k: (i, 0, 0), memory_space=pltpu.SMEM),
                pl.BlockSpec((None, n, d), lambda i, k: (i, 0, 0)),
                pl.BlockSpec((None, e, n), lambda i, k: (i, 0, 0)),
                pl.BlockSpec((None, e, n), lambda i, k: (i, 0, 0)),
                _single((None, n, d), lambda i, k: (i, 0, 0)),
                pl.BlockSpec((None, 1, d), lambda i, k: (i * N_MOD + 5, 0, 0)),
                pl.BlockSpec((None, d, f), lambda i, k: (layer * e + k, 0, 0)),
                pl.BlockSpec((None, d, f), lambda i, k: (layer * e + k, 0, 0)),
                pl.BlockSpec((None, f, d), lambda i, k: (layer * e + k, 0, 0))]
    args = [idx, h2, pos, gate, x1, mods, w1, w3, w2]
    if final:
        in_specs.append(pl.BlockSpec((1, d), lambda i, k: (0, 0)))
        args.append(final_g.reshape(1, d))
    return pl.pallas_call(
        functools.partial(_moe_kernel, cap=cap, final=final),
        grid=(b, e),
        in_specs=in_specs,
        out_specs=pl.BlockSpec((None, n, d), lambda i, k: (i, 0, 0)),
        out_shape=jax.ShapeDtypeStruct((b, n, d), F32),
        scratch_shapes=[pltpu.VMEM((cap, d), F32)],
        compiler_params=_params(2),
        name="moe",
    )(*args)


def _vconv_kernel(u_ref, w_ref, b_ref, lg_ref, lb_ref, o_ref, pad_s, *, n, k_taps):
    d = u_ref.shape[-1]
    halo = (k_taps - 1) // 2 * GRID_W
    tail = k_taps // 2 * GRID_W
    pad_s[0:halo, :] = jnp.zeros((halo, d), F32)
    pad_s[halo + n:halo + n + tail, :] = jnp.zeros((tail, d), F32)
    tcp = min(256, n)

    def fill(ci, carry):
        s = pl.multiple_of(ci * tcp, tcp)
        pad_s[pl.ds(halo + s, tcp), :] = u_ref[pl.ds(s, tcp), :].astype(F32)
        return carry

    lax.fori_loop(0, n // tcp, fill, 0)

    def body(r, carry):
        acc = jnp.zeros((GRID_W, d), F32)
        for k in range(k_taps):
            s = pl.multiple_of((r + k) * GRID_W, GRID_W)
            acc = acc + pad_s[pl.ds(s, GRID_W), :] * w_ref[k:k + 1, :]
        acc = acc + b_ref[...]
        mu = jnp.mean(acc, axis=-1, keepdims=True)
        xc = acc - mu
        var = jnp.mean(xc * xc, axis=-1, keepdims=True)
        y = xc * lax.rsqrt(var + EPS) * lg_ref[...] + lb_ref[...]
        o_ref[pl.ds(pl.multiple_of(r * GRID_W, GRID_W), GRID_W), :] = _silu(y).astype(o_ref.dtype)
        return carry

    lax.fori_loop(0, n // GRID_W, body, 0)


def _vconv_ln_silu(u, dw_w, dw_b, ln_g, ln_b):
    b, n, d = u.shape
    k_taps = dw_w.shape[0]
    full = lambda shape: pl.BlockSpec(shape, lambda i: tuple(0 for _ in shape))
    return pl.pallas_call(
        functools.partial(_vconv_kernel, n=n, k_taps=k_taps),
        grid=(b,),
        in_specs=[pl.BlockSpec((None, n, d), lambda i: (i, 0, 0)),
                  full((k_taps, d)), full((1, d)), full((1, d)), full((1, d))],
        out_specs=pl.BlockSpec((None, n, d), lambda i: (i, 0, 0)),
        out_shape=jax.ShapeDtypeStruct((b, n, d), BF16),
        scratch_shapes=[pltpu.VMEM((n + (k_taps - 1) * GRID_W, d), F32)],
        compiler_params=_params(1),
        name="vconv_ln_silu",
    )(u, dw_w, dw_b.reshape(1, d), ln_g.reshape(1, d), ln_b.reshape(1, d))


def kernel(x, c, ctx, c_ctx, norm1_g, norm2_g, w_mod, b_mod, w_in, conv_a_w, conv_a_b, lru_wr, lru_br, lru_wi, lru_bi, lru_lam, conv_b_w, conv_b_b, filt_w1, filt_b1, filt_w2, filt_b2, filt_w3, filt_b3, filt_freq, filt_bias, w_out, cf_w1, cf_b1, cf_dw_w, cf_dw_b, cf_ln_g, cf_ln_b, cf_w2, cf_b2, router, exp_w1, exp_w3, exp_w2, final_g):
    bsz, n, d = x.shape
    depth = w_mod.shape[0]
    assert depth == 2 and w_in.shape[0] == 1 and cf_w1.shape[0] == 1
    rw = conv_a_w.shape[-1]
    hw = filt_bias.shape[-1]
    e = router.shape[-1]
    cap = EC_CAPACITY * n // e

    rows = -(-(bsz + 1) // SUBLANES) * SUBLANES
    cvec = jnp.concatenate([c, c_ctx[None], jnp.zeros((rows - bsz - 1, d), F32)], axis=0)
    mods_all = _adaln(cvec, w_mod, b_mod)
    mods = [mods_all[l].reshape(rows * N_MOD, 1, d) for l in range(depth)]

    w_in0 = w_in[0].astype(BF16)
    wg, bg = _lru_gate_weights(lru_wr[0], lru_wi[0], lru_br[0], lru_bi[0])
    ctx_xa = _norm_matmul(ctx, norm1_g[0], mods[0], lambda i: bsz, 0, 1, w_in0[:, rw:2 * rw],
                          out_dtype=BF16)
    h0_ctx = jnp.zeros((bsz, 2, rw), F32)
    _, ctx_state = _rglru(ctx_xa, 0, None, 0, conv_a_w[0], conv_a_b[0], wg, bg, lru_lam[0], h0_ctx)
    proj = _norm_matmul(x, norm1_g[0], mods[0], lambda i: i, 0, 1, w_in0, out_dtype=BF16)
    y_a, _ = _rglru(proj, 1, proj, 0, conv_a_w[0], conv_a_b[0], wg, bg, lru_lam[0], ctx_state,
                    out_dtype=BF16)

    fsum, fdif, kn = _hyena_filters(n, filt_w1[0], filt_b1[0], filt_w2[0], filt_b2[0],
                                    filt_w3[0], filt_b3[0], filt_freq[0])
    cm, sm = (jnp.asarray(t) for t in _dft_tables(n))
    cm_hi, sm_hi = cm.astype(BF16), sm.astype(BF16)
    cm_lo = (cm - cm_hi.astype(F32)).astype(BF16)
    sm_lo = (sm - sm_hi.astype(F32)).astype(BF16)
    kr, ki = _hyena_spectrum(cm_hi, cm_lo, sm_hi, sm_lo, fsum, fdif)
    z = _hyena(proj, 2 * rw, conv_b_w[0], conv_b_b[0], cm_hi, sm_hi,
               kr, ki, kn, filt_bias[0], out_dtype=BF16)

    ff = exp_w1.shape[-1]
    ew1, ew3, ew2 = _cast_bf16(exp_w1.reshape(depth * e, d, ff), exp_w3.reshape(depth * e, d, ff),
                               exp_w2.reshape(depth * e, ff, d))

    x1, h2, logits = _outproj([y_a, z], w_out[0].astype(BF16), None, x, mods[0], norm2_g[0], router[0])
    pos, gate, idx = _route(logits, cap)
    xl = _moe(h2, pos, gate, idx, x1, mods[0], ew1, ew3, ew2, 0)

    u = _norm_matmul(xl, norm1_g[1], mods[1], lambda i: i, 0, 1, cf_w1[0].astype(BF16),
                     bias=cf_b1[0], glu=True, out_dtype=BF16)
    v = _vconv_ln_silu(u, cf_dw_w[0], cf_dw_b[0], cf_ln_g[0], cf_ln_b[0])
    x1, h2, logits = _outproj([v], cf_w2[0].astype(BF16), cf_b2[0], xl, mods[1], norm2_g[1], router[1])
    pos, gate, idx = _route(logits, cap)
    return _moe(h2, pos, gate, idx, x1, mods[1], ew1, ew3, ew2, 1, final_g=final_g)
```

```python
import functools
import math

import numpy as np
import jax
import jax.numpy as jnp
from jax import lax
from jax.experimental import pallas as pl
from jax.experimental.pallas import tpu as pltpu

F32 = jnp.float32
BF16 = jnp.bfloat16
HIGHEST = lax.Precision.HIGHEST

EPS = 1e-6
GRID_W = 64
N_MOD = 6
LRU_HEADS = 8
LRU_C = 8.0
HYENA_ORDER = 2
HYENA_DECAY_TARGET = 1e-2
HYENA_FAST_PCT = 0.3
HYENA_SLOW_PCT = 1.5
HYENA_SHIFT = 0.05
EC_CAPACITY = 2

V7X_VMEM_BYTES = 64 * 1024 * 1024
VMEM_LIMIT = V7X_VMEM_BYTES - 4 * 1024 * 1024
SUBLANES = 8
LANES = 128


def _params(n_axes):
    return pltpu.CompilerParams(dimension_semantics=("arbitrary",) * n_axes,
                                vmem_limit_bytes=VMEM_LIMIT)


def _single(block_shape, index_map):
    return pl.BlockSpec(block_shape, index_map, pipeline_mode=pl.Buffered(1))


def _silu(x):
    return x * jax.nn.sigmoid(x)


def _split_hi_lo(x):
    hi = x.astype(BF16)
    lo = (x - hi.astype(F32)).astype(BF16)
    return jnp.concatenate([hi, lo], axis=-1)


def _norm_mod(xf, g, shift, scale):
    ms = jnp.mean(xf * xf, axis=-1, keepdims=True)
    y = xf * lax.rsqrt(ms + EPS) * g
    return y * (1.0 + scale) + shift


def _adaln_kernel(c_ref, w_ref, b_ref, o_ref):
    s = _silu(c_ref[...])
    o_ref[...] = jnp.dot(s, w_ref[...], precision=HIGHEST, preferred_element_type=F32) + b_ref[...]


def _adaln(cvec, w_mod, b_mod, tc=512):
    depth, d, dm = w_mod.shape
    r = cvec.shape[0]
    return pl.pallas_call(
        _adaln_kernel,
        grid=(depth, dm // tc),
        in_specs=[pl.BlockSpec((r, d), lambda l, j: (0, 0)),
                  pl.BlockSpec((None, d, tc), lambda l, j: (l, 0, j)),
                  pl.BlockSpec((None, 1, tc), lambda l, j: (l, 0, j))],
        out_specs=pl.BlockSpec((None, r, tc), lambda l, j: (l, 0, j)),
        out_shape=jax.ShapeDtypeStruct((depth, r, dm), F32),
        compiler_params=_params(2),
        name="adaln",
    )(cvec, w_mod, b_mod.reshape(depth, 1, dm))


def _norm_matmul_kernel(x_ref, g_ref, sh_ref, sc_ref, w_ref, *rest, has_bias, glu):
    o_ref = rest[-1]
    h = _norm_mod(x_ref[...], g_ref[...], sh_ref[...], sc_ref[...])
    y = jnp.dot(h.astype(BF16), w_ref[...], preferred_element_type=F32)
    if has_bias:
        y = y + rest[0][...]
    if glu:
        half = y.shape[-1] // 2
        y = y[:, :half] * jax.nn.sigmoid(y[:, half:])
    o_ref[...] = y.astype(o_ref.dtype)


def _norm_matmul(x, g, mods, mod_row, shift_j, scale_j, w_bf16, bias=None, glu=False, tn=512,
                 out_dtype=F32):
    b, n, d = x.shape
    nout = w_bf16.shape[1]
    tn = min(tn, n)
    out_w = nout // 2 if glu else nout
    in_specs = [pl.BlockSpec((None, tn, d), lambda i, j: (i, j, 0)),
                pl.BlockSpec((1, d), lambda i, j: (0, 0)),
                pl.BlockSpec((None, 1, d), lambda i, j: (mod_row(i) * N_MOD + shift_j, 0, 0)),
                pl.BlockSpec((None, 1, d), lambda i, j: (mod_row(i) * N_MOD + scale_j, 0, 0)),
                _single((d, nout), lambda i, j: (0, 0))]
    args = [x, g.reshape(1, d), mods, mods, w_bf16]
    if bias is not None:
        in_specs.append(pl.BlockSpec((1, nout), lambda i, j: (0, 0)))
        args.append(bias.reshape(1, nout))
    return pl.pallas_call(
        functools.partial(_norm_matmul_kernel, has_bias=bias is not None, glu=glu),
        grid=(b, n // tn),
        in_specs=in_specs,
        out_specs=pl.BlockSpec((None, tn, out_w), lambda i, j: (i, j, 0)),
        out_shape=jax.ShapeDtypeStruct((b, n, out_w), out_dtype),
        compiler_params=_params(2),
        name="norm_matmul",
    )(*args)


HALO = 16


def _shift_rows(cur, k, prev, nxt, has_prev, has_next):
    t = cur.shape[0]
    row = lax.broadcasted_iota(jnp.int32, cur.shape, 0)
    if k < 0:
        out = pltpu.roll(cur, -k, 0)
        for r in range(-k):
            src = prev[HALO + k + r:HALO + k + r + 1, :]
            fill = jnp.where(has_prev, src, 0.0)
            out = jnp.where(row == r, fill, out)
    else:
        out = pltpu.roll(cur, t - k, 0)
        for r in range(k):
            src = nxt[r:r + 1, :]
            fill = jnp.where(has_next, src, 0.0)
            out = jnp.where(row == t - k + r, fill, out)
    return out


def _dwconv_chunk(x_ref, s, t, n, w_ref, b_ref, pad_left):
    cur = x_ref[pl.ds(s, t), :].astype(F32)
    prev = x_ref[pl.ds(pl.multiple_of(jnp.maximum(s - HALO, 0), HALO), HALO), :].astype(F32)
    nxt = x_ref[pl.ds(pl.multiple_of(jnp.minimum(s + t, n - HALO), HALO), HALO), :].astype(F32)
    has_prev = s > 0
    has_next = s + t < n
    k_taps = w_ref.shape[0]
    acc = None
    for k in range(k_taps):
        off = k - pad_left
        xs = cur if off == 0 else _shift_rows(cur, off, prev, nxt, has_prev, has_next)
        term = xs * w_ref[k:k + 1, :]
        acc = term if acc is None else acc + term
    return acc + b_ref[...]


GATE_CHUNK = 256
SCAN_CHUNK = 64


def _scan_rows(a, b, reverse):
    t = a.shape[0]
    row = lax.broadcasted_iota(jnp.int32, a.shape, 0)
    s = 1
    while s < t:
        if reverse:
            valid = row < t - s
            sh = t - s
        else:
            valid = row >= s
            sh = s
        a_s = jnp.where(valid, pltpu.roll(a, sh, 0), 1.0)
        b_s = jnp.where(valid, pltpu.roll(b, sh, 0), 0.0)
        b = a * b_s + b
        a = a * a_s
        s *= 2
    return a, b


def _gelu_tanh(x):
    return 0.5 * x * (1.0 + jnp.tanh(math.sqrt(2.0 / math.pi) * (x + 0.044715 * (x * x * x))))


def _rglru_kernel(*refs, n, with_gate):
    if with_gate:
        (xa_ref, gate_ref, cw_ref, cb_ref, wg_ref, bg_ref, lam_ref, h0_ref,
         y_ref, hl_ref, af_s, uf_s, ab_s, ub_s) = refs
    else:
        (xa_ref, cw_ref, cb_ref, wg_ref, bg_ref, lam_ref, h0_ref,
         hl_ref, af_s, uf_s, ab_s, ub_s) = refs
    w = xa_ref.shape[-1]
    tg = min(GATE_CHUNK, n)
    ts = min(SCAN_CHUNK, n)
    lam = lam_ref[...]
    sp = jnp.maximum(-lam, 0.0) + jnp.log(1.0 + jnp.exp(-jnp.abs(lam)))

    def gates(ci, carry):
        s = pl.multiple_of(ci * tg, tg)
        xa = _dwconv_chunk(xa_ref, s, tg, n, cw_ref, cb_ref, 1)
        z = jnp.dot(xa.astype(BF16), wg_ref[...], preferred_element_type=F32) + bg_ref[...]
        for d, (a_s, u_s) in enumerate(((af_s, uf_s), (ab_s, ub_s))):
            r = 0.5 * jnp.tanh(0.5 * z[:, (2 * d) * w:(2 * d + 1) * w]) + 0.5
            i = 0.5 * jnp.tanh(0.5 * z[:, (2 * d + 1) * w:(2 * d + 2) * w]) + 0.5
            log_a = (-LRU_C) * r * sp[d:d + 1, :]
            a_s[pl.ds(s, tg), :] = jnp.exp(log_a)
            u_s[pl.ds(s, tg), :] = jnp.sqrt(1.0 - jnp.exp(2.0 * log_a)) * i * xa
        return carry

    lax.fori_loop(0, n // tg, gates, 0)

    nc = n // ts

    def fwd(ci, carry):
        s = pl.multiple_of(ci * ts, ts)
        a_c, h_l = _scan_rows(af_s[pl.ds(s, ts), :], uf_s[pl.ds(s, ts), :], False)
        h = h_l + a_c * carry
        uf_s[pl.ds(s, ts), :] = h
        return h[ts - 1:ts, :]

    hf_last = lax.fori_loop(0, nc, fwd, h0_ref[0:1, :])

    def bwd(ci, carry):
        s = pl.multiple_of((nc - 1 - ci) * ts, ts)
        a_c, h_l = _scan_rows(ab_s[pl.ds(s, ts), :], ub_s[pl.ds(s, ts), :], True)
        h = h_l + a_c * carry
        if with_gate:
            hsum = uf_s[pl.ds(s, ts), :] + h
            gate = gate_ref[pl.ds(s, ts), :].astype(F32)
            y_ref[pl.ds(s, ts), :] = (_gelu_tanh(gate) * hsum).astype(y_ref.dtype)
        return h[0:1, :]

    hb_last = lax.fori_loop(0, nc, bwd, h0_ref[1:2, :])
    hl_ref[0:1, :] = hf_last
    hl_ref[1:2, :] = hb_last


def _rglru(xa_src, xa_col, gate_src, gate_col, conv_w, conv_b, wg_bf16, bg, lam, h0, out_dtype=F32):
    b, n, _ = xa_src.shape
    w = conv_w.shape[1]
    with_gate = gate_src is not None
    seq = lambda col: pl.BlockSpec((None, n, w), lambda i: (i, 0, col))
    full = lambda shape: pl.BlockSpec(shape, lambda i: tuple(0 for _ in shape))
    in_specs = [seq(xa_col)]
    args = [xa_src]
    if with_gate:
        in_specs.append(seq(gate_col))
        args.append(gate_src)
    in_specs += [full(conv_w.shape), full((1, w)), _single(wg_bf16.shape, lambda i: (0, 0)),
                 full((1, 4 * w)), full((2, w)), pl.BlockSpec((None, 2, w), lambda i: (i, 0, 0))]
    args += [conv_w, conv_b.reshape(1, w), wg_bf16, bg.reshape(1, 4 * w), lam, h0]
    out_specs = [pl.BlockSpec((None, 2, w), lambda i: (i, 0, 0))]
    out_shape = [jax.ShapeDtypeStruct((b, 2, w), F32)]
    if with_gate:
        out_specs.insert(0, pl.BlockSpec((None, n, w), lambda i: (i, 0, 0)))
        out_shape.insert(0, jax.ShapeDtypeStruct((b, n, w), out_dtype))
    outs = pl.pallas_call(
        functools.partial(_rglru_kernel, n=n, with_gate=with_gate),
        grid=(b,),
        in_specs=in_specs,
        out_specs=out_specs,
        out_shape=out_shape,
        scratch_shapes=[pltpu.VMEM((n, w), F32)] * 4,
        compiler_params=_params(1),
        name="rglru_gated" if with_gate else "rglru_state",
    )(*args)
    return (outs[0], outs[1]) if with_gate else (None, outs[0])


def _lru_gate_weights(lru_wr, lru_wi, lru_br, lru_bi):
    def dense(wh):
        heads, hd, _ = wh.shape
        eye = jnp.eye(heads, dtype=wh.dtype)
        return jnp.einsum('hij,hg->higj', wh, eye).reshape(heads * hd, heads * hd)
    wg = jnp.concatenate([dense(lru_wr[0]), dense(lru_wi[0]), dense(lru_wr[1]), dense(lru_wi[1])], axis=1)
    bg = jnp.concatenate([lru_br[0], lru_bi[0], lru_br[1], lru_bi[1]], axis=0)
    return wg.astype(BF16), bg


@functools.lru_cache(maxsize=None)
def _dft_tables(n):
    m = 2 * n
    k = (np.arange(n, dtype=np.int64)[:, None] * np.arange(n, dtype=np.int64)[None, :]) % m
    ang = 2.0 * np.pi * np.arange(m, dtype=np.float64) / m
    cm = np.cos(ang)[k].astype(np.float32)
    sm = np.sin(ang)[k].astype(np.float32)
    return cm, sm


@functools.lru_cache(maxsize=None)
def _filter_features(n, emb):
    t = np.linspace(0.0, 1.0, n, dtype=np.float32)[:, None]
    bands = (emb - 1) // 2
    w = (np.float32(2.0 * math.pi / n) * np.arange(n, dtype=np.float32))[:, None]
    f = np.linspace(1e-4, bands - 1, bands, dtype=np.float32)[None, :]
    z = np.concatenate([t, np.cos(f * w), -np.sin(f * w)], axis=-1).astype(np.float32)
    return z


@functools.lru_cache(maxsize=None)
def _hyena_deltas(width):
    d = np.abs(np.linspace(math.log(HYENA_DECAY_TARGET) / HYENA_SLOW_PCT,
                           math.log(HYENA_DECAY_TARGET) / HYENA_FAST_PCT, width, dtype=np.float32))
    return d.reshape(1, width).astype(np.float32)


def _filter_kernel(z_ref, t_ref, w1_ref, b1_ref, w2_ref, b2_ref, w3_ref, b3_ref, fr_ref, dl_ref,
                   fwd_ref, bwd_ref):
    hw = dl_ref.shape[-1]
    hid = jnp.sin(fr_ref[0:1, :] * (jnp.dot(z_ref[...], w1_ref[...], precision=HIGHEST,
                                            preferred_element_type=F32) + b1_ref[...]))
    hid = jnp.sin(fr_ref[1:2, :] * (jnp.dot(hid, w2_ref[...], precision=HIGHEST,
                                            preferred_element_type=F32) + b2_ref[...]))
    filt = jnp.dot(hid, w3_ref[...], precision=HIGHEST, preferred_element_type=F32) + b3_ref[...]
    window = jnp.exp(-t_ref[...] * dl_ref[...]) + HYENA_SHIFT
    for o in range(HYENA_ORDER):
        fwd_ref[o] = filt[:, (2 * o) * hw:(2 * o + 1) * hw] * window
        bwd_ref[o] = filt[:, (2 * o + 1) * hw:(2 * o + 2) * hw] * window


def _hyena_filters(n, filt_w1, filt_b1, filt_w2, filt_b2, filt_w3, filt_b3, filt_freq, tn=256):
    emb, hidden = filt_w1.shape
    hw = filt_w3.shape[1] // (2 * HYENA_ORDER)
    tn = min(tn, n)
    z = jnp.asarray(_filter_features(n, emb))
    t = jnp.asarray(np.linspace(0.0, 1.0, n, dtype=np.float32)[:, None])
    dl = jnp.asarray(_hyena_deltas(hw))
    full = lambda shape: pl.BlockSpec(shape, lambda j: tuple(0 for _ in shape))
    return pl.pallas_call(
        _filter_kernel,
        grid=(n // tn,),
        in_specs=[pl.BlockSpec((tn, emb), lambda j: (j, 0)),
                  pl.BlockSpec((tn, 1), lambda j: (j, 0)),
                  full((emb, hidden)), full((1, hidden)), full((hidden, hidden)), full((1, hidden)),
                  full((hidden, 2 * HYENA_ORDER * hw)), full((1, 2 * HYENA_ORDER * hw)),
                  full((2, hidden)), full((1, hw))],
        out_specs=[pl.BlockSpec((HYENA_ORDER, tn, hw), lambda j: (0, j, 0))] * 2,
        out_shape=[jax.ShapeDtypeStruct((HYENA_ORDER, n, hw), F32)] * 2,
        compiler_params=_params(1),
        name="hyena_filter",
    )(z, t, filt_w1, filt_b1.reshape(1, -1), filt_w2, filt_b2.reshape(1, -1),
      filt_w3, filt_b3.reshape(1, -1), filt_freq, dl)


def _dot_hi_lo(t_hi, t_lo, f_cat):
    hw = f_cat.shape[-1] // 2
    p = jnp.dot(t_hi, f_cat, preferred_element_type=F32)
    return p[:, :hw] + p[:, hw:] + jnp.dot(t_lo, f_cat[:, :hw], preferred_element_type=F32)


HYENA_PHASES = 4


def _polyphase_filters(fwd, bwd):
    r = HYENA_PHASES
    order, n, hw = fwd.shape
    fp = fwd.reshape(order, n // r, r, hw)
    bp = bwd.reshape(order, n // r, r, hw)

    def shift1(a, first):
        return jnp.concatenate([first[:, None], a[:, :-1]], axis=1)

    def no_lag0(a):
        return a.at[:, 0].set(0.0)

    zero = jnp.zeros((order, hw), fwd.dtype)
    pos, neg = [], []
    for d in range(-(r - 1), r):
        if d == 0:
            pos.append(fp[:, :, 0])
            neg.append(no_lag0(bp[:, :, 0]))
        elif d > 0:
            pos.append(fp[:, :, d])
            neg.append(shift1(bp[:, :, r - d], zero))
        else:
            pos.append(shift1(fp[:, :, r + d], bp[:, 0, -d]))
            neg.append(no_lag0(bp[:, :, -d]))
    return jnp.stack(pos, axis=1), jnp.stack(neg, axis=1)


def _spectrum_kernel(ch_ref, cl_ref, sh_ref, sl_ref, kp_ref, km_ref, kr_ref, ki_ref, kn_ref):
    m = kp_ref.shape[0]
    ksum = kp_ref[...] + km_ref[...]
    kdif = km_ref[...] - kp_ref[...]
    row = lax.broadcasted_iota(jnp.int32, ksum.shape, 0)
    scale = jnp.where(row == 0, 1.0 / (2 * m), 2.0 / (2 * m))
    kr_ref[...] = scale * _dot_hi_lo(ch_ref[...], cl_ref[...], _split_hi_lo(ksum))
    ki_ref[...] = scale * _dot_hi_lo(sh_ref[...], sl_ref[...], _split_hi_lo(kdif))
    alt = jnp.where((row & 1) == 0, 1.0, -1.0)
    kn_ref[...] = jnp.sum(ksum * alt, axis=0, keepdims=True) * (1.0 / (2 * m))


def _hyena_spectrum(c_hi, c_lo, s_hi, s_lo, kpos, kneg):
    order, nd, m, hw = kpos.shape
    tab = pl.BlockSpec((m, m), lambda o, d: (0, 0))
    filt = pl.BlockSpec((None, None, m, hw), lambda o, d: (o, d, 0, 0))
    row = pl.BlockSpec((None, None, 1, hw), lambda o, d: (o, d, 0, 0))
    return pl.pallas_call(
        _spectrum_kernel,
        grid=(order, nd),
        in_specs=[tab, tab, tab, tab, filt, filt],
        out_specs=[filt, filt, row],
        out_shape=[jax.ShapeDtypeStruct((order, nd, m, hw), F32)] * 2
        + [jax.ShapeDtypeStruct((order, nd, 1, hw), F32)],
        compiler_params=_params(2),
        name="hyena_spectrum",
    )(c_hi, c_lo, s_hi, s_lo, kpos, kneg)


HYENA_CHUNK = 512


def _hyena_kernel(v_ref, g0_ref, g1_ref, cwv_ref, cw0_ref, cw1_ref, cbv_ref, cb0_ref, cb1_ref,
                  cm_ref, sm_ref, kr_ref, ki_ref, kn_ref, fb_ref, z_ref,
                  u_s, g_s, x_s, *, n):
    r = HYENA_PHASES
    ct = v_ref.shape[-1]
    m = n // r
    tc = min(HYENA_CHUNK, n)
    row = lax.broadcasted_iota(jnp.int32, (m, ct), 0)
    alt = jnp.where((row & 1) == 0, 1.0, -1.0)
    rc = min(128, m)

    groups = ct // LANES

    def put_rows(dst_s, rows, val):
        for j in range(groups):
            dst_s[j, rows, :] = val[:, j * LANES:(j + 1) * LANES]

    def get_rows(src_s, rows):
        return jnp.concatenate([src_s[j, rows, :] for j in range(groups)], axis=1)

    def short_conv(src_ref, cw_ref, cb_ref, dst_s):
        def body(ci, carry):
            s = pl.multiple_of(ci * tc, tc)
            put_rows(dst_s, pl.ds(s, tc), _dwconv_chunk(src_ref, s, tc, n, cw_ref, cb_ref, 1))
            return carry
        lax.fori_loop(0, n // tc, body, 0)

    def phase(ref, p):
        return get_rows(ref, pl.ds(p, m, stride=r))

    short_conv(v_ref, cwv_ref, cbv_ref, u_s)
    for o, (g_ref, cw_ref, cb_ref) in enumerate(((g0_ref, cw0_ref, cb0_ref), (g1_ref, cw1_ref, cb1_ref))):
        short_conv(g_ref, cw_ref, cb_ref, g_s)
        nyq = []
        for q in range(r):
            uq = phase(u_s, q)
            ub = uq.astype(BF16)
            x_s[q, 0] = jnp.dot(cm_ref[...], ub, preferred_element_type=F32)
            x_s[q, 1] = jnp.dot(sm_ref[...], ub, preferred_element_type=F32)
            nyq.append(jnp.sum(uq * alt, axis=0, keepdims=True))
        for p in range(r):
            zc_parts, zs_parts = [], []
            for r0 in range(0, m, rc):
                zc = zs = None
                for q in range(r):
                    d = p - q + r - 1
                    xre = x_s[q, 0, r0:r0 + rc, :]
                    xim = x_s[q, 1, r0:r0 + rc, :]
                    kr = kr_ref[o, d, r0:r0 + rc, :]
                    ki = ki_ref[o, d, r0:r0 + rc, :]
                    tc_ = xre * kr + xim * ki
                    ts_ = xim * kr - xre * ki
                    zc = tc_ if zc is None else zc + tc_
                    zs = ts_ if zs is None else zs + ts_
                zc_parts.append(zc.astype(BF16))
                zs_parts.append(zs.astype(BF16))
            zc = jnp.concatenate(zc_parts, axis=0)
            zs = jnp.concatenate(zs_parts, axis=0)
            y = (jnp.dot(cm_ref[...], zc, preferred_element_type=F32)
                 + jnp.dot(sm_ref[...], zs, preferred_element_type=F32))
            ynyq = None
            for q in range(r):
                t = nyq[q] * kn_ref[o, p - q + r - 1]
                ynyq = t if ynyq is None else ynyq + t
            y = y + alt * ynyq + phase(u_s, p) * fb_ref[o]
            put_rows(u_s, pl.ds(p, m, stride=r), phase(g_s, p) * y)

    def emit(ci, carry):
        rows = pl.ds(pl.multiple_of(ci * tc, tc), tc)
        z_ref[rows, :] = get_rows(u_s, rows).astype(z_ref.dtype)
        return carry

    lax.fori_loop(0, n // tc, emit, 0)


def _hyena(proj, hy_col0, conv_w, conv_b, cm_bf16, sm_bf16, kr, ki, kn, filt_bias, ct=256,
           out_dtype=F32):
    b, n, _ = proj.shape
    hw = kr.shape[-1]
    nd, m = kr.shape[1], kr.shape[2]
    ct = min(ct, hw)
    nct = hw // ct
    base = hy_col0 // ct
    k_taps = conv_w.shape[0]

    def seq(j):
        return pl.BlockSpec((None, n, ct), lambda c, i: (i, 0, base + j * nct + c))

    def cw(j):
        return pl.BlockSpec((k_taps, ct), lambda c, i: (0, j * nct + c))

    def cb(j):
        return pl.BlockSpec((1, ct), lambda c, i: (0, j * nct + c))

    kspec = _single((HYENA_ORDER, nd, m, ct), lambda c, i: (0, 0, 0, c))
    knspec = pl.BlockSpec((HYENA_ORDER, nd, 1, ct), lambda c, i: (0, 0, 0, c))
    rowspec = pl.BlockSpec((HYENA_ORDER, 1, ct), lambda c, i: (0, 0, c))
    conv_b2 = conv_b.reshape(1, -1)
    return pl.pallas_call(
        functools.partial(_hyena_kernel, n=n),
        grid=(nct, b),
        in_specs=[seq(0), seq(1), seq(2), cw(0), cw(1), cw(2), cb(0), cb(1), cb(2),
                  pl.BlockSpec((m, m), lambda c, i: (0, 0)), pl.BlockSpec((m, m), lambda c, i: (0, 0)),
                  kspec, kspec, knspec, rowspec],
        out_specs=pl.BlockSpec((None, n, ct), lambda c, i: (i, 0, c)),
        out_shape=jax.ShapeDtypeStruct((b, n, hw), out_dtype),
        scratch_shapes=[pltpu.VMEM((ct // LANES, n, LANES), F32), pltpu.VMEM((ct // LANES, n, LANES), F32),
                        pltpu.VMEM((HYENA_PHASES, 2, m, ct), F32)],
        compiler_params=_params(2),
        name="hyena",
    )(proj, proj, proj, conv_w, conv_w, conv_w, conv_b2, conv_b2, conv_b2,
      cm_bf16, sm_bf16, kr, ki, kn, filt_bias.reshape(HYENA_ORDER, 1, hw))


def _outproj_kernel(*refs, n_act, has_bias):
    acts = refs[:n_act]
    (w_ref, x_ref, g1_ref, ng_ref, sh_ref, sc_ref, rt_ref) = refs[n_act:n_act + 7]
    rest = refs[n_act + 7:]
    x1_ref, h2_ref, lg_ref = rest[-3:]
    y = None
    k0 = 0
    for a_ref in acts:
        k = a_ref.shape[-1]
        part = jnp.dot(a_ref[...].astype(BF16), w_ref[k0:k0 + k, :], preferred_element_type=F32)
        y = part if y is None else y + part
        k0 += k
    if has_bias:
        y = y + rest[0][...]
    x1 = x_ref[...] + g1_ref[...] * y
    x1_ref[...] = x1
    h2 = _norm_mod(x1, ng_ref[...], sh_ref[...], sc_ref[...])
    h_hi = h2.astype(BF16)
    h_lo = (h2 - h_hi.astype(F32)).astype(BF16)
    h2_ref[...] = h2
    ne = lg_ref.shape[0]
    nt = (((1,), (1,)), ((), ()))
    p_hi = lax.dot_general(rt_ref[...], h_hi, nt, preferred_element_type=F32)
    p_lo = lax.dot_general(rt_ref[0:ne, :], h_lo, nt, preferred_element_type=F32)
    lg_ref[...] = p_hi[0:ne, :] + p_hi[ne:2 * ne, :] + p_lo


def _outproj(acts, w_bf16, bias, x, mods, norm_g, router, tn=512):
    b, n, d = x.shape
    e = router.shape[1]
    tn = min(tn, n)
    row = lambda j: pl.BlockSpec((None, 1, d), lambda i, t: (i * N_MOD + j, 0, 0))
    in_specs = [pl.BlockSpec((None, tn, a.shape[-1]), lambda i, t: (i, t, 0)) for a in acts]
    in_specs += [_single(w_bf16.shape, lambda i, t: (0, 0)),
                 pl.BlockSpec((None, tn, d), lambda i, t: (i, t, 0)),
                 row(2), pl.BlockSpec((1, d), lambda i, t: (0, 0)), row(3), row(4),
                 pl.BlockSpec((2 * e, d), lambda i, t: (0, 0))]
    rt = router.T
    rt_hi = rt.astype(BF16)
    rt_lo = (rt - rt_hi.astype(F32)).astype(BF16)
    args = list(acts) + [w_bf16, x, mods, norm_g.reshape(1, d), mods, mods,
                         jnp.concatenate([rt_hi, rt_lo], axis=0)]
    if bias is not None:
        in_specs.append(pl.BlockSpec((1, d), lambda i, t: (0, 0)))
        args.append(bias.reshape(1, d))
    return pl.pallas_call(
        functools.partial(_outproj_kernel, n_act=len(acts), has_bias=bias is not None),
        grid=(b, n // tn),
        in_specs=in_specs,
        out_specs=[pl.BlockSpec((None, tn, d), lambda i, t: (i, t, 0)),
                   pl.BlockSpec((None, tn, d), lambda i, t: (i, t, 0)),
                   pl.BlockSpec((None, e, tn), lambda i, t: (i, 0, t))],
        out_shape=[jax.ShapeDtypeStruct((b, n, d), F32),
                   jax.ShapeDtypeStruct((b, n, d), F32),
                   jax.ShapeDtypeStruct((b, e, n), F32)],
        compiler_params=_params(2),
        name="outproj_router",
    )(*args)


def _cumsum_lanes(x):
    n = x.shape[-1]
    lane = lax.broadcasted_iota(jnp.int32, x.shape, 1)
    s = 1
    while s < n:
        x = x + jnp.where(lane >= s, pltpu.roll(x, s, 1), 0)
        s *= 2
    return x


def _compact_lanes(slot, sel):
    n = slot.shape[-1]
    lane = lax.broadcasted_iota(jnp.int32, slot.shape, 1)
    val = lane
    enc = jnp.where(sel, ((lane - slot) << 1) | 1, 0)
    j = 0
    while (1 << j) < n:
        sh = 1 << j
        inside = lane < n - sh
        src_val = pltpu.roll(val, n - sh, 1)
        src_enc = jnp.where(inside, pltpu.roll(enc, n - sh, 1), 0)
        take = ((src_enc >> (j + 1)) & 1) > 0
        stay = (enc & (1 - ((enc >> (j + 1)) & 1))) > 0
        val = jnp.where(take, src_val, val)
        enc = jnp.where(take, src_enc, jnp.where(stay, enc, 0))
        j += 1
    return val


def _route_kernel(lg_ref, pos_ref, gate_ref, idx_ref, *, cap):
    for s in range(lg_ref.shape[0]):
        _route_one(lg_ref.at[s], pos_ref.at[s], gate_ref.at[s], idx_ref.at[s], cap)


def _route_one(lg_ref, pos_ref, gate_ref, idx_ref, cap):
    lg = lg_ref[...]
    m = jnp.max(lg, axis=0, keepdims=True)
    ex = jnp.exp(lg - m)
    probs = ex / jnp.sum(ex, axis=0, keepdims=True)
    thr = jnp.zeros((lg.shape[0], 1), jnp.int32)
    for bit in range(30, -1, -1):
        cand = thr | (1 << bit)
        cnt = jnp.sum(jnp.where(probs >= pltpu.bitcast(cand, F32), 1.0, 0.0), axis=1, keepdims=True)
        thr = jnp.where(cnt >= cap, cand, thr)
    gt = probs >= pltpu.bitcast(thr + 1, F32)
    eq = jnp.logical_and(probs >= pltpu.bitcast(thr, F32), jnp.logical_not(gt))
    need = cap - jnp.sum(jnp.where(gt, 1.0, 0.0), axis=1, keepdims=True)
    eq_rank = _cumsum_lanes(jnp.where(eq, 1.0, 0.0))
    sel = jnp.where(gt, 1.0, jnp.where(eq, jnp.where(eq_rank <= need, 1.0, 0.0), 0.0))
    slot = (_cumsum_lanes(sel) - 1.0).astype(jnp.int32)
    chosen = sel > 0.0
    pos_ref[...] = jnp.where(chosen, slot, -1)
    gate_ref[...] = jnp.where(chosen, probs, 0.0)
    idx_ref[...] = _compact_lanes(slot, chosen)[:, :idx_ref.shape[-1]]


ROUTE_SAMPLES = 4


def _route(logits, cap):
    b, e, n = logits.shape
    g = math.gcd(ROUTE_SAMPLES, b)
    capw = -(-cap // 128) * 128
    spec = pl.BlockSpec((g, e, n), lambda i: (i, 0, 0))
    return pl.pallas_call(
        functools.partial(_route_kernel, cap=cap),
        grid=(b // g,),
        in_specs=[spec],
        out_specs=[spec, spec, pl.BlockSpec((g, e, capw), lambda i: (i, 0, 0))],
        out_shape=[jax.ShapeDtypeStruct((b, e, n), jnp.int32), jax.ShapeDtypeStruct((b, e, n), F32),
                   jax.ShapeDtypeStruct((b, e, capw), jnp.int32)],
        compiler_params=_params(1),
        name="route",
    )(logits)


MOE_ROW_CHUNK = 512


def _moe_kernel(idx_ref, h_ref, pos_ref, gate_ref, x_ref, g2_ref, w1_ref, w3_ref, w2_ref, *rest,
                cap, final):
    n_extra = 1 if final else 0
    o_ref, xs_s = rest[n_extra:n_extra + 2]
    e = pl.program_id(1)
    ne = pl.num_programs(1)
    n = h_ref.shape[0]
    tr = min(MOE_ROW_CHUNK, n)

    def gather_rows(k):
        for c in range(cap):
            xs_s[c:c + 1, :] = h_ref[pl.ds(idx_ref[k, c], 1), :]

    @pl.when(e == 0)
    def _():
        o_ref[...] = x_ref[...]
        gather_rows(0)

    xs = xs_s[...].astype(BF16)
    gather_rows(jnp.minimum(e + 1, ne - 1))
    pos = pos_ref[pl.ds(e, 1), :]
    gate = gate_ref[pl.ds(e, 1), :]
    hit = lax.broadcasted_iota(jnp.int32, (cap, n), 0) == pos
    onehot = jnp.where(hit, 1.0, 0.0).astype(BF16)
    a = jnp.dot(xs, w1_ref[...], preferred_element_type=F32)
    g = jnp.dot(xs, w3_ref[...], preferred_element_type=F32)
    y = jnp.dot((_silu(a) * g).astype(BF16), w2_ref[...], preferred_element_type=F32)
    slot_gate = jnp.sum(jnp.where(hit, gate, 0.0), axis=1, keepdims=True)
    ys = (y * slot_gate * g2_ref[...]).astype(BF16)
    for r0 in range(0, n, tr):
        o_ref[r0:r0 + tr, :] += lax.dot_general(onehot[:, r0:r0 + tr], ys, (((0,), (0,)), ((), ())),
                                                preferred_element_type=F32)

    if final:
        @pl.when(e == ne - 1)
        def _():
            def norm_rows(ci, carry):
                rows = pl.ds(pl.multiple_of(ci * tr, tr), tr)
                xf = o_ref[rows, :]
                ms = jnp.mean(xf * xf, axis=-1, keepdims=True)
                o_ref[rows, :] = xf * lax.rsqrt(ms + EPS) * rest[0][...]
                return carry
            lax.fori_loop(0, n // tr, norm_rows, 0)


def _cast_kernel(*refs):
    k = len(refs) // 2
    for x_ref, o_ref in zip(refs[:k], refs[k:]):
        o_ref[...] = x_ref[...].astype(o_ref.dtype)


def _cast_bf16(*ws):
    g = ws[0].shape[0]
    specs = [pl.BlockSpec((None,) + w.shape[1:], lambda i: (i, 0, 0)) for w in ws]
    return pl.pallas_call(
        _cast_kernel,
        grid=(g,),
        in_specs=specs,
        out_specs=specs,
        out_shape=[jax.ShapeDtypeStruct(w.shape, BF16) for w in ws],
        compiler_params=_params(1),
        name="cast_bf16",
    )(*ws)


def _moe(h2, pos, gate, idx, x1, mods, w1, w3, w2, layer, final_g=None):
    b, n, d = x1.shape
    e = pos.shape[1]
    f = w1.shape[-1]
    cap = EC_CAPACITY * n // e
    final = final_g is not None
    in_specs = [pl.BlockSpec((None, e, idx.shape[-1]), lambda i, k: (i, 0, 0), memory_space=pltpu.SMEM),
                pl.BlockSpec((None, n, d), lambda i, k: (i, 0, 0)),
                pl.BlockSpec((None, e, n), lambda i, k: (i, 0, 0)),
                pl.BlockSpec((None, e, n), lambda i, k: (i, 0, 0)),
                _single((None, n, d), lambda i, k: (i, 0, 0)),
                pl.BlockSpec((None, 1, d), lambda i, k: (i * N_MOD + 5, 0, 0)),
                pl.BlockSpec((None, d, f), lambda i, k: (layer * e + k, 0, 0)),
                pl.BlockSpec((None, d, f), lambda i, k: (layer * e + k, 0, 0)),
                pl.BlockSpec((None, f, d), lambda i, k: (layer * e + k, 0, 0))]
    args = [idx, h2, pos, gate, x1, mods, w1, w3, w2]
    if final:
        in_specs.append(pl.BlockSpec((1, d), lambda i, k: (0, 0)))
        args.append(final_g.reshape(1, d))
    return pl.pallas_call(
        functools.partial(_moe_kernel, cap=cap, final=final),
        grid=(b, e),
        in_specs=in_specs,
        out_specs=pl.BlockSpec((None, n, d), lambda i, k: (i, 0, 0)),
        out_shape=jax.ShapeDtypeStruct((b, n, d), F32),
        scratch_shapes=[pltpu.VMEM((cap, d), F32)],
        compiler_params=_params(2),
        name="moe",
    )(*args)


VCONV_GROUP = 4


def _vconv_kernel(u_ref, w_ref, b_ref, lg_ref, lb_ref, o_ref, pad_s, y_s, *, n, k_taps, group):
    d = u_ref.shape[-1]
    groups = d // LANES
    n_rows = n // GRID_W
    top = (k_taps - 1) // 2
    halo = (group - 1) * GRID_W
    span = group * GRID_W
    tcp = min(256, n)

    for j in range(groups):
        if halo:
            pad_s[j, 0:halo, :] = jnp.zeros((halo, LANES), F32)
            pad_s[j, halo + n:halo + n + halo, :] = jnp.zeros((halo, LANES), F32)

    def fill(ci, carry):
        s = pl.multiple_of(ci * tcp, tcp)
        x = u_ref[pl.ds(s, tcp), :].astype(F32)
        for j in range(groups):
            pad_s[j, pl.ds(halo + s, tcp), :] = x[:, j * LANES:(j + 1) * LANES]
        return carry

    lax.fori_loop(0, n // tcp, fill, 0)

    for g in range(n_rows // group):
        r0 = g * group
        taps = [k for k in range(k_taps) if any(0 <= r0 + j + k - top < n_rows for j in range(group))]

        def lane_group(cj, carry, r0=r0, taps=taps):
            acc = None
            for k in taps:
                s = (r0 + k - top) * GRID_W + halo
                term = pad_s[cj, s:s + span, :] * w_ref[cj, k:k + 1, :]
                acc = term if acc is None else acc + term
            y_s[cj] = acc + b_ref[cj]
            return carry

        lax.fori_loop(0, groups, lane_group, 0)
        for j in range(group):
            acc = jnp.concatenate([y_s[cj, j * GRID_W:(j + 1) * GRID_W, :] for cj in range(groups)], axis=1)
            mu = jnp.mean(acc, axis=-1, keepdims=True)
            xc = acc - mu
            var = jnp.mean(xc * xc, axis=-1, keepdims=True)
            y = xc * lax.rsqrt(var + EPS) * lg_ref[...] + lb_ref[...]
            o_ref[(r0 + j) * GRID_W:(r0 + j + 1) * GRID_W, :] = _silu(y).astype(o_ref.dtype)


def _vconv_ln_silu(u, dw_w, dw_b, ln_g, ln_b):
    b, n, d = u.shape
    k_taps = dw_w.shape[0]
    groups = d // LANES
    group = math.gcd(VCONV_GROUP, n // GRID_W)
    halo = (group - 1) * GRID_W
    full = lambda shape: pl.BlockSpec(shape, lambda i: tuple(0 for _ in shape))
    w3 = dw_w.reshape(k_taps, groups, LANES).transpose(1, 0, 2)
    b3 = dw_b.reshape(groups, 1, LANES)
    return pl.pallas_call(
        functools.partial(_vconv_kernel, n=n, k_taps=k_taps, group=group),
        grid=(b,),
        in_specs=[pl.BlockSpec((None, n, d), lambda i: (i, 0, 0)),
                  full((groups, k_taps, LANES)), full((groups, 1, LANES)), full((1, d)), full((1, d))],
        out_specs=pl.BlockSpec((None, n, d), lambda i: (i, 0, 0)),
        out_shape=jax.ShapeDtypeStruct((b, n, d), BF16),
        scratch_shapes=[pltpu.VMEM((groups, n + 2 * halo, LANES), F32),
                        pltpu.VMEM((groups, group * GRID_W, LANES), F32)],
        compiler_params=_params(1),
        name="vconv_ln_silu",
    )(u, w3, b3, ln_g.reshape(1, d), ln_b.reshape(1, d))


def kernel(x, c, ctx, c_ctx, norm1_g, norm2_g, w_mod, b_mod, w_in, conv_a_w, conv_a_b, lru_wr, lru_br, lru_wi, lru_bi, lru_lam, conv_b_w, conv_b_b, filt_w1, filt_b1, filt_w2, filt_b2, filt_w3, filt_b3, filt_freq, filt_bias, w_out, cf_w1, cf_b1, cf_dw_w, cf_dw_b, cf_ln_g, cf_ln_b, cf_w2, cf_b2, router, exp_w1, exp_w3, exp_w2, final_g):
    bsz, n, d = x.shape
    depth = w_mod.shape[0]
    assert depth == 2 and w_in.shape[0] == 1 and cf_w1.shape[0] == 1
    rw = conv_a_w.shape[-1]
    hw = filt_bias.shape[-1]
    e = router.shape[-1]
    cap = EC_CAPACITY * n // e

    rows = -(-(bsz + 1) // SUBLANES) * SUBLANES
    cvec = jnp.concatenate([c, c_ctx[None], jnp.zeros((rows - bsz - 1, d), F32)], axis=0)
    mods_all = _adaln(cvec, w_mod, b_mod)
    mods = [mods_all[l].reshape(rows * N_MOD, 1, d) for l in range(depth)]

    w_in0 = w_in[0].astype(BF16)
    wg, bg = _lru_gate_weights(lru_wr[0], lru_wi[0], lru_br[0], lru_bi[0])
    ctx_xa = _norm_matmul(ctx, norm1_g[0], mods[0], lambda i: bsz, 0, 1, w_in0[:, rw:2 * rw],
                          out_dtype=BF16)
    h0_ctx = jnp.zeros((bsz, 2, rw), F32)
    _, ctx_state = _rglru(ctx_xa, 0, None, 0, conv_a_w[0], conv_a_b[0], wg, bg, lru_lam[0], h0_ctx)
    proj = _norm_matmul(x, norm1_g[0], mods[0], lambda i: i, 0, 1, w_in0, out_dtype=BF16)
    y_a, _ = _rglru(proj, 1, proj, 0, conv_a_w[0], conv_a_b[0], wg, bg, lru_lam[0], ctx_state,
                    out_dtype=BF16)

    fwd, bwd = _hyena_filters(n, filt_w1[0], filt_b1[0], filt_w2[0], filt_b2[0],
                              filt_w3[0], filt_b3[0], filt_freq[0])
    kpos, kneg = _polyphase_filters(fwd, bwd)
    cm, sm = (jnp.asarray(t) for t in _dft_tables(n // HYENA_PHASES))
    cm_hi, sm_hi = cm.astype(BF16), sm.astype(BF16)
    cm_lo = (cm - cm_hi.astype(F32)).astype(BF16)
    sm_lo = (sm - sm_hi.astype(F32)).astype(BF16)
    kr, ki, kn = _hyena_spectrum(cm_hi, cm_lo, sm_hi, sm_lo, kpos, kneg)
    z = _hyena(proj, 2 * rw, conv_b_w[0], conv_b_b[0], cm_hi, sm_hi,
               kr, ki, kn, filt_bias[0], out_dtype=BF16)

    ff = exp_w1.shape[-1]
    ew1, ew3, ew2 = _cast_bf16(exp_w1.reshape(depth * e, d, ff), exp_w3.reshape(depth * e, d, ff),
                               exp_w2.reshape(depth * e, ff, d))

    x1, h2, logits = _outproj([y_a, z], w_out[0].astype(BF16), None, x, mods[0], norm2_g[0], router[0])
    pos, gate, idx = _route(logits, cap)
    xl = _moe(h2, pos, gate, idx, x1, mods[0], ew1, ew3, ew2, 0)

    u = _norm_matmul(xl, norm1_g[1], mods[1], lambda i: i, 0, 1, cf_w1[0].astype(BF16),
                     bias=cf_b1[0], glu=True, out_dtype=BF16)
    v = _vconv_ln_silu(u, cf_dw_w[0], cf_dw_b[0], cf_ln_g[0], cf_ln_b[0])
    x1, h2, logits = _outproj([v], cf_w2[0].astype(BF16), cf_b2[0], xl, mods[1], norm2_g[1], router[1])
    pos, gate, idx = _route(logits, cap)
    return _moe(h2, pos, gate, idx, x1, mods[1], ew1, ew3, ew2, 1, final_g=final_g)
```

```python
import functools
import math

import numpy as np
import jax
import jax.numpy as jnp
from jax import lax
from jax.experimental import pallas as pl
from jax.experimental.pallas import tpu as pltpu

F32 = jnp.float32
BF16 = jnp.bfloat16
HIGHEST = lax.Precision.HIGHEST

EPS = 1e-6
GRID_W = 64
N_MOD = 6
LRU_HEADS = 8
LRU_C = 8.0
HYENA_ORDER = 2
HYENA_DECAY_TARGET = 1e-2
HYENA_FAST_PCT = 0.3
HYENA_SLOW_PCT = 1.5
HYENA_SHIFT = 0.05
EC_CAPACITY = 2

V7X_VMEM_BYTES = 64 * 1024 * 1024
VMEM_LIMIT = V7X_VMEM_BYTES - 4 * 1024 * 1024
SUBLANES = 8
LANES = 128


def _params(n_axes):
    return pltpu.CompilerParams(dimension_semantics=("arbitrary",) * n_axes,
                                vmem_limit_bytes=VMEM_LIMIT)


def _single(block_shape, index_map):
    return pl.BlockSpec(block_shape, index_map, pipeline_mode=pl.Buffered(1))


def _silu(x):
    return x * jax.nn.sigmoid(x)


def _split_hi_lo(x):
    hi = x.astype(BF16)
    lo = (x - hi.astype(F32)).astype(BF16)
    return jnp.concatenate([hi, lo], axis=-1)


def _norm_mod(xf, g, shift, scale):
    ms = jnp.mean(xf * xf, axis=-1, keepdims=True)
    y = xf * lax.rsqrt(ms + EPS) * g
    return y * (1.0 + scale) + shift


def _adaln_kernel(c_ref, w_ref, b_ref, o_ref):
    s = _silu(c_ref[...])
    o_ref[...] = jnp.dot(s, w_ref[...], precision=HIGHEST, preferred_element_type=F32) + b_ref[...]


def _adaln(cvec, w_mod, b_mod, tc=512):
    depth, d, dm = w_mod.shape
    r = cvec.shape[0]
    return pl.pallas_call(
        _adaln_kernel,
        grid=(depth, dm // tc),
        in_specs=[pl.BlockSpec((r, d), lambda l, j: (0, 0)),
                  pl.BlockSpec((None, d, tc), lambda l, j: (l, 0, j)),
                  pl.BlockSpec((None, 1, tc), lambda l, j: (l, 0, j))],
        out_specs=pl.BlockSpec((None, r, tc), lambda l, j: (l, 0, j)),
        out_shape=jax.ShapeDtypeStruct((depth, r, dm), F32),
        compiler_params=_params(2),
        name="adaln",
    )(cvec, w_mod, b_mod.reshape(depth, 1, dm))


def _norm_matmul_kernel(x_ref, g_ref, sh_ref, sc_ref, w_ref, *rest, has_bias, glu):
    o_ref = rest[-1]
    h = _norm_mod(x_ref[...], g_ref[...], sh_ref[...], sc_ref[...])
    y = jnp.dot(h.astype(BF16), w_ref[...], preferred_element_type=F32)
    if has_bias:
        y = y + rest[0][...]
    if glu:
        half = y.shape[-1] // 2
        y = y[:, :half] * jax.nn.sigmoid(y[:, half:])
    o_ref[...] = y.astype(o_ref.dtype)


def _norm_matmul(x, g, mods, mod_row, shift_j, scale_j, w_bf16, bias=None, glu=False, tn=512,
                 out_dtype=F32):
    b, n, d = x.shape
    nout = w_bf16.shape[1]
    tn = min(tn, n)
    out_w = nout // 2 if glu else nout
    in_specs = [pl.BlockSpec((None, tn, d), lambda i, j: (i, j, 0)),
                pl.BlockSpec((1, d), lambda i, j: (0, 0)),
                pl.BlockSpec((None, 1, d), lambda i, j: (mod_row(i) * N_MOD + shift_j, 0, 0)),
                pl.BlockSpec((None, 1, d), lambda i, j: (mod_row(i) * N_MOD + scale_j, 0, 0)),
                _single((d, nout), lambda i, j: (0, 0))]
    args = [x, g.reshape(1, d), mods, mods, w_bf16]
    if bias is not None:
        in_specs.append(pl.BlockSpec((1, nout), lambda i, j: (0, 0)))
        args.append(bias.reshape(1, nout))
    return pl.pallas_call(
        functools.partial(_norm_matmul_kernel, has_bias=bias is not None, glu=glu),
        grid=(b, n // tn),
        in_specs=in_specs,
        out_specs=pl.BlockSpec((None, tn, out_w), lambda i, j: (i, j, 0)),
        out_shape=jax.ShapeDtypeStruct((b, n, out_w), out_dtype),
        compiler_params=_params(2),
        name="norm_matmul",
    )(*args)


HALO = 16


def _shift_rows(cur, k, prev, nxt, has_prev, has_next):
    t = cur.shape[0]
    row = lax.broadcasted_iota(jnp.int32, cur.shape, 0)
    if k < 0:
        out = pltpu.roll(cur, -k, 0)
        for r in range(-k):
            src = prev[HALO + k + r:HALO + k + r + 1, :]
            fill = jnp.where(has_prev, src, 0.0)
            out = jnp.where(row == r, fill, out)
    else:
        out = pltpu.roll(cur, t - k, 0)
        for r in range(k):
            src = nxt[r:r + 1, :]
            fill = jnp.where(has_next, src, 0.0)
            out = jnp.where(row == t - k + r, fill, out)
    return out


def _dwconv_chunk(x_ref, s, t, n, w_ref, b_ref, pad_left):
    cur = x_ref[pl.ds(s, t), :].astype(F32)
    prev = x_ref[pl.ds(pl.multiple_of(jnp.maximum(s - HALO, 0), HALO), HALO), :].astype(F32)
    nxt = x_ref[pl.ds(pl.multiple_of(jnp.minimum(s + t, n - HALO), HALO), HALO), :].astype(F32)
    has_prev = s > 0
    has_next = s + t < n
    k_taps = w_ref.shape[0]
    acc = None
    for k in range(k_taps):
        off = k - pad_left
        xs = cur if off == 0 else _shift_rows(cur, off, prev, nxt, has_prev, has_next)
        term = xs * w_ref[k:k + 1, :]
        acc = term if acc is None else acc + term
    return acc + b_ref[...]


GATE_CHUNK = 256
SCAN_CHUNK = 64


def _scan_rows(a, b, reverse):
    t = a.shape[0]
    row = lax.broadcasted_iota(jnp.int32, a.shape, 0)
    s = 1
    while s < t:
        if reverse:
            valid = row < t - s
            sh = t - s
        else:
            valid = row >= s
            sh = s
        a_s = jnp.where(valid, pltpu.roll(a, sh, 0), 1.0)
        b_s = jnp.where(valid, pltpu.roll(b, sh, 0), 0.0)
        b = a * b_s + b
        a = a * a_s
        s *= 2
    return a, b


def _gelu_tanh(x):
    return 0.5 * x * (1.0 + jnp.tanh(math.sqrt(2.0 / math.pi) * (x + 0.044715 * (x * x * x))))


def _rglru_kernel(*refs, n, with_gate):
    if with_gate:
        (xa_ref, gate_ref, cw_ref, cb_ref, wg_ref, bg_ref, lam_ref, h0_ref,
         y_ref, hl_ref, af_s, uf_s, ab_s, ub_s) = refs
    else:
        (xa_ref, cw_ref, cb_ref, wg_ref, bg_ref, lam_ref, h0_ref,
         hl_ref, af_s, uf_s, ab_s, ub_s) = refs
    w = xa_ref.shape[-1]
    tg = min(GATE_CHUNK, n)
    ts = min(SCAN_CHUNK, n)
    lam = lam_ref[...]
    sp = jnp.maximum(-lam, 0.0) + jnp.log(1.0 + jnp.exp(-jnp.abs(lam)))

    def gates(ci, carry):
        s = pl.multiple_of(ci * tg, tg)
        xa = _dwconv_chunk(xa_ref, s, tg, n, cw_ref, cb_ref, 1)
        z = jnp.dot(xa.astype(BF16), wg_ref[...], preferred_element_type=F32) + bg_ref[...]
        for d, (a_s, u_s) in enumerate(((af_s, uf_s), (ab_s, ub_s))):
            r = 0.5 * jnp.tanh(0.5 * z[:, (2 * d) * w:(2 * d + 1) * w]) + 0.5
            i = 0.5 * jnp.tanh(0.5 * z[:, (2 * d + 1) * w:(2 * d + 2) * w]) + 0.5
            log_a = (-LRU_C) * r * sp[d:d + 1, :]
            a_s[pl.ds(s, tg), :] = jnp.exp(log_a)
            u_s[pl.ds(s, tg), :] = jnp.sqrt(1.0 - jnp.exp(2.0 * log_a)) * i * xa
        return carry

    lax.fori_loop(0, n // tg, gates, 0)

    nc = n // ts

    def fwd(ci, carry):
        s = pl.multiple_of(ci * ts, ts)
        a_c, h_l = _scan_rows(af_s[pl.ds(s, ts), :], uf_s[pl.ds(s, ts), :], False)
        h = h_l + a_c * carry
        uf_s[pl.ds(s, ts), :] = h
        return h[ts - 1:ts, :]

    hf_last = lax.fori_loop(0, nc, fwd, h0_ref[0:1, :])

    def bwd(ci, carry):
        s = pl.multiple_of((nc - 1 - ci) * ts, ts)
        a_c, h_l = _scan_rows(ab_s[pl.ds(s, ts), :], ub_s[pl.ds(s, ts), :], True)
        h = h_l + a_c * carry
        if with_gate:
            hsum = uf_s[pl.ds(s, ts), :] + h
            gate = gate_ref[pl.ds(s, ts), :].astype(F32)
            y_ref[pl.ds(s, ts), :] = (_gelu_tanh(gate) * hsum).astype(y_ref.dtype)
        return h[0:1, :]

    hb_last = lax.fori_loop(0, nc, bwd, h0_ref[1:2, :])
    hl_ref[0:1, :] = hf_last
    hl_ref[1:2, :] = hb_last


def _rglru(xa_src, xa_col, gate_src, gate_col, conv_w, conv_b, wg_bf16, bg, lam, h0, out_dtype=F32):
    b, n, _ = xa_src.shape
    w = conv_w.shape[1]
    with_gate = gate_src is not None
    seq = lambda col: pl.BlockSpec((None, n, w), lambda i: (i, 0, col))
    full = lambda shape: pl.BlockSpec(shape, lambda i: tuple(0 for _ in shape))
    in_specs = [seq(xa_col)]
    args = [xa_src]
    if with_gate:
        in_specs.append(seq(gate_col))
        args.append(gate_src)
    in_specs += [full(conv_w.shape), full((1, w)), _single(wg_bf16.shape, lambda i: (0, 0)),
                 full((1, 4 * w)), full((2, w)), pl.BlockSpec((None, 2, w), lambda i: (i, 0, 0))]
    args += [conv_w, conv_b.reshape(1, w), wg_bf16, bg.reshape(1, 4 * w), lam, h0]
    out_specs = [pl.BlockSpec((None, 2, w), lambda i: (i, 0, 0))]
    out_shape = [jax.ShapeDtypeStruct((b, 2, w), F32)]
    if with_gate:
        out_specs.insert(0, pl.BlockSpec((None, n, w), lambda i: (i, 0, 0)))
        out_shape.insert(0, jax.ShapeDtypeStruct((b, n, w), out_dtype))
    outs = pl.pallas_call(
        functools.partial(_rglru_kernel, n=n, with_gate=with_gate),
        grid=(b,),
        in_specs=in_specs,
        out_specs=out_specs,
        out_shape=out_shape,
        scratch_shapes=[pltpu.VMEM((n, w), F32)] * 4,
        compiler_params=_params(1),
        name="rglru_gated" if with_gate else "rglru_state",
    )(*args)
    return (outs[0], outs[1]) if with_gate else (None, outs[0])


def _lru_gate_weights(lru_wr, lru_wi, lru_br, lru_bi):
    def dense(wh):
        heads, hd, _ = wh.shape
        eye = jnp.eye(heads, dtype=wh.dtype)
        return jnp.einsum('hij,hg->higj', wh, eye).reshape(heads * hd, heads * hd)
    wg = jnp.concatenate([dense(lru_wr[0]), dense(lru_wi[0]), dense(lru_wr[1]), dense(lru_wi[1])], axis=1)
    bg = jnp.concatenate([lru_br[0], lru_bi[0], lru_br[1], lru_bi[1]], axis=0)
    return wg.astype(BF16), bg


@functools.lru_cache(maxsize=None)
def _dft_tables(n):
    m = 2 * n
    k = (np.arange(n, dtype=np.int64)[:, None] * np.arange(n, dtype=np.int64)[None, :]) % m
    ang = 2.0 * np.pi * np.arange(m, dtype=np.float64) / m
    cm = np.cos(ang)[k].astype(np.float32)
    sm = np.sin(ang)[k].astype(np.float32)
    return cm, sm


@functools.lru_cache(maxsize=None)
def _filter_features(n, emb):
    t = np.linspace(0.0, 1.0, n, dtype=np.float32)[:, None]
    bands = (emb - 1) // 2
    w = (np.float32(2.0 * math.pi / n) * np.arange(n, dtype=np.float32))[:, None]
    f = np.linspace(1e-4, bands - 1, bands, dtype=np.float32)[None, :]
    z = np.concatenate([t, np.cos(f * w), -np.sin(f * w)], axis=-1).astype(np.float32)
    return z


@functools.lru_cache(maxsize=None)
def _hyena_deltas(width):
    d = np.abs(np.linspace(math.log(HYENA_DECAY_TARGET) / HYENA_SLOW_PCT,
                           math.log(HYENA_DECAY_TARGET) / HYENA_FAST_PCT, width, dtype=np.float32))
    return d.reshape(1, width).astype(np.float32)


def _filter_kernel(z_ref, t_ref, w1_ref, b1_ref, w2_ref, b2_ref, w3_ref, b3_ref, fr_ref, dl_ref,
                   fwd_ref, bwd_ref):
    hw = dl_ref.shape[-1]
    hid = jnp.sin(fr_ref[0:1, :] * (jnp.dot(z_ref[...], w1_ref[...], precision=HIGHEST,
                                            preferred_element_type=F32) + b1_ref[...]))
    hid = jnp.sin(fr_ref[1:2, :] * (jnp.dot(hid, w2_ref[...], precision=HIGHEST,
                                            preferred_element_type=F32) + b2_ref[...]))
    filt = jnp.dot(hid, w3_ref[...], precision=HIGHEST, preferred_element_type=F32) + b3_ref[...]
    window = jnp.exp(-t_ref[...] * dl_ref[...]) + HYENA_SHIFT
    for o in range(HYENA_ORDER):
        fwd_ref[o] = filt[:, (2 * o) * hw:(2 * o + 1) * hw] * window
        bwd_ref[o] = filt[:, (2 * o + 1) * hw:(2 * o + 2) * hw] * window


def _hyena_filters(n, filt_w1, filt_b1, filt_w2, filt_b2, filt_w3, filt_b3, filt_freq, tn=256):
    emb, hidden = filt_w1.shape
    hw = filt_w3.shape[1] // (2 * HYENA_ORDER)
    tn = min(tn, n)
    z = jnp.asarray(_filter_features(n, emb))
    t = jnp.asarray(np.linspace(0.0, 1.0, n, dtype=np.float32)[:, None])
    dl = jnp.asarray(_hyena_deltas(hw))
    full = lambda shape: pl.BlockSpec(shape, lambda j: tuple(0 for _ in shape))
    return pl.pallas_call(
        _filter_kernel,
        grid=(n // tn,),
        in_specs=[pl.BlockSpec((tn, emb), lambda j: (j, 0)),
                  pl.BlockSpec((tn, 1), lambda j: (j, 0)),
                  full((emb, hidden)), full((1, hidden)), full((hidden, hidden)), full((1, hidden)),
                  full((hidden, 2 * HYENA_ORDER * hw)), full((1, 2 * HYENA_ORDER * hw)),
                  full((2, hidden)), full((1, hw))],
        out_specs=[pl.BlockSpec((HYENA_ORDER, tn, hw), lambda j: (0, j, 0))] * 2,
        out_shape=[jax.ShapeDtypeStruct((HYENA_ORDER, n, hw), F32)] * 2,
        compiler_params=_params(1),
        name="hyena_filter",
    )(z, t, filt_w1, filt_b1.reshape(1, -1), filt_w2, filt_b2.reshape(1, -1),
      filt_w3, filt_b3.reshape(1, -1), filt_freq, dl)


def _dot_hi_lo(t_hi, t_lo, f_cat):
    hw = f_cat.shape[-1] // 2
    p = jnp.dot(t_hi, f_cat, preferred_element_type=F32)
    return p[:, :hw] + p[:, hw:] + jnp.dot(t_lo, f_cat[:, :hw], preferred_element_type=F32)


HYENA_PHASES = 4


def _polyphase_filters(fwd, bwd):
    r = HYENA_PHASES
    order, n, hw = fwd.shape
    fp = fwd.reshape(order, n // r, r, hw)
    bp = bwd.reshape(order, n // r, r, hw)

    def shift1(a, first):
        return jnp.concatenate([first[:, None], a[:, :-1]], axis=1)

    def no_lag0(a):
        return a.at[:, 0].set(0.0)

    zero = jnp.zeros((order, hw), fwd.dtype)
    pos, neg = [], []
    for d in range(-(r - 1), r):
        if d == 0:
            pos.append(fp[:, :, 0])
            neg.append(no_lag0(bp[:, :, 0]))
        elif d > 0:
            pos.append(fp[:, :, d])
            neg.append(shift1(bp[:, :, r - d], zero))
        else:
            pos.append(shift1(fp[:, :, r + d], bp[:, 0, -d]))
            neg.append(no_lag0(bp[:, :, -d]))
    return jnp.stack(pos, axis=1), jnp.stack(neg, axis=1)


def _spectrum_kernel(ch_ref, cl_ref, sh_ref, sl_ref, kp_ref, km_ref, kr_ref, ki_ref, kn_ref):
    m = kp_ref.shape[0]
    ksum = kp_ref[...] + km_ref[...]
    kdif = km_ref[...] - kp_ref[...]
    row = lax.broadcasted_iota(jnp.int32, ksum.shape, 0)
    scale = jnp.where(row == 0, 1.0 / (2 * m), 2.0 / (2 * m))
    kr_ref[...] = scale * _dot_hi_lo(ch_ref[...], cl_ref[...], _split_hi_lo(ksum))
    ki_ref[...] = scale * _dot_hi_lo(sh_ref[...], sl_ref[...], _split_hi_lo(kdif))
    alt = jnp.where((row & 1) == 0, 1.0, -1.0)
    kn_ref[...] = jnp.sum(ksum * alt, axis=0, keepdims=True) * (1.0 / (2 * m))


def _hyena_spectrum(c_hi, c_lo, s_hi, s_lo, kpos, kneg):
    order, nd, m, hw = kpos.shape
    tab = pl.BlockSpec((m, m), lambda o, d: (0, 0))
    filt = pl.BlockSpec((None, None, m, hw), lambda o, d: (o, d, 0, 0))
    row = pl.BlockSpec((None, None, 1, hw), lambda o, d: (o, d, 0, 0))
    return pl.pallas_call(
        _spectrum_kernel,
        grid=(order, nd),
        in_specs=[tab, tab, tab, tab, filt, filt],
        out_specs=[filt, filt, row],
        out_shape=[jax.ShapeDtypeStruct((order, nd, m, hw), F32)] * 2
        + [jax.ShapeDtypeStruct((order, nd, 1, hw), F32)],
        compiler_params=_params(2),
        name="hyena_spectrum",
    )(c_hi, c_lo, s_hi, s_lo, kpos, kneg)


HYENA_CHUNK = 512


def _hyena_kernel(v_ref, g0_ref, g1_ref, cwv_ref, cw0_ref, cw1_ref, cbv_ref, cb0_ref, cb1_ref,
                  cm_ref, sm_ref, kr_ref, ki_ref, kn_ref, fb_ref, z_ref,
                  u_s, g_s, x_s, *, n):
    r = HYENA_PHASES
    ct = v_ref.shape[-1]
    m = n // r
    tc = min(HYENA_CHUNK, n)
    row = lax.broadcasted_iota(jnp.int32, (m, ct), 0)
    alt = jnp.where((row & 1) == 0, 1.0, -1.0)
    rc = min(128, m)

    groups = ct // LANES

    def put_rows(dst_s, rows, val):
        for j in range(groups):
            dst_s[j, rows, :] = val[:, j * LANES:(j + 1) * LANES]

    def get_rows(src_s, rows):
        return jnp.concatenate([src_s[j, rows, :] for j in range(groups)], axis=1)

    def short_conv(src_ref, cw_ref, cb_ref, dst_s):
        def body(ci, carry):
            s = pl.multiple_of(ci * tc, tc)
            put_rows(dst_s, pl.ds(s, tc), _dwconv_chunk(src_ref, s, tc, n, cw_ref, cb_ref, 1))
            return carry
        lax.fori_loop(0, n // tc, body, 0)

    def phase(ref, p):
        return get_rows(ref, pl.ds(p, m, stride=r))

    short_conv(v_ref, cwv_ref, cbv_ref, u_s)
    for o, (g_ref, cw_ref, cb_ref) in enumerate(((g0_ref, cw0_ref, cb0_ref), (g1_ref, cw1_ref, cb1_ref))):
        short_conv(g_ref, cw_ref, cb_ref, g_s)
        nyq = []
        for q in range(r):
            uq = phase(u_s, q)
            ub = uq.astype(BF16)
            x_s[q, 0] = jnp.dot(cm_ref[...], ub, preferred_element_type=F32)
            x_s[q, 1] = jnp.dot(sm_ref[...], ub, preferred_element_type=F32)
            nyq.append(jnp.sum(uq * alt, axis=0, keepdims=True))
        for p in range(r):
            zc_parts, zs_parts = [], []
            for r0 in range(0, m, rc):
                zc = zs = None
                for q in range(r):
                    d = p - q + r - 1
                    xre = x_s[q, 0, r0:r0 + rc, :]
                    xim = x_s[q, 1, r0:r0 + rc, :]
                    kr = kr_ref[o, d, r0:r0 + rc, :]
                    ki = ki_ref[o, d, r0:r0 + rc, :]
                    tc_ = xre * kr + xim * ki
                    ts_ = xim * kr - xre * ki
                    zc = tc_ if zc is None else zc + tc_
                    zs = ts_ if zs is None else zs + ts_
                zc_parts.append(zc.astype(BF16))
                zs_parts.append(zs.astype(BF16))
            zc = jnp.concatenate(zc_parts, axis=0)
            zs = jnp.concatenate(zs_parts, axis=0)
            y = (jnp.dot(cm_ref[...], zc, preferred_element_type=F32)
                 + jnp.dot(sm_ref[...], zs, preferred_element_type=F32))
            ynyq = None
            for q in range(r):
                t = nyq[q] * kn_ref[o, p - q + r - 1]
                ynyq = t if ynyq is None else ynyq + t
            y = y + alt * ynyq + phase(u_s, p) * fb_ref[o]
            put_rows(u_s, pl.ds(p, m, stride=r), phase(g_s, p) * y)

    def emit(ci, carry):
        rows = pl.ds(pl.multiple_of(ci * tc, tc), tc)
        z_ref[rows, :] = get_rows(u_s, rows).astype(z_ref.dtype)
        return carry

    lax.fori_loop(0, n // tc, emit, 0)


def _hyena(proj, hy_col0, conv_w, conv_b, cm_bf16, sm_bf16, kr, ki, kn, filt_bias, ct=256,
           out_dtype=F32):
    b, n, _ = proj.shape
    hw = kr.shape[-1]
    nd, m = kr.shape[1], kr.shape[2]
    ct = min(ct, hw)
    nct = hw // ct
    base = hy_col0 // ct
    k_taps = conv_w.shape[0]

    def seq(j):
        return pl.BlockSpec((None, n, ct), lambda c, i: (i, 0, base + j * nct + c))

    def cw(j):
        return pl.BlockSpec((k_taps, ct), lambda c, i: (0, j * nct + c))

    def cb(j):
        return pl.BlockSpec((1, ct), lambda c, i: (0, j * nct + c))

    kspec = _single((HYENA_ORDER, nd, m, ct), lambda c, i: (0, 0, 0, c))
    knspec = pl.BlockSpec((HYENA_ORDER, nd, 1, ct), lambda c, i: (0, 0, 0, c))
    rowspec = pl.BlockSpec((HYENA_ORDER, 1, ct), lambda c, i: (0, 0, c))
    conv_b2 = conv_b.reshape(1, -1)
    return pl.pallas_call(
        functools.partial(_hyena_kernel, n=n),
        grid=(nct, b),
        in_specs=[seq(0), seq(1), seq(2), cw(0), cw(1), cw(2), cb(0), cb(1), cb(2),
                  pl.BlockSpec((m, m), lambda c, i: (0, 0)), pl.BlockSpec((m, m), lambda c, i: (0, 0)),
                  kspec, kspec, knspec, rowspec],
        out_specs=pl.BlockSpec((None, n, ct), lambda c, i: (i, 0, c)),
        out_shape=jax.ShapeDtypeStruct((b, n, hw), out_dtype),
        scratch_shapes=[pltpu.VMEM((ct // LANES, n, LANES), F32), pltpu.VMEM((ct // LANES, n, LANES), F32),
                        pltpu.VMEM((HYENA_PHASES, 2, m, ct), F32)],
        compiler_params=_params(2),
        name="hyena",
    )(proj, proj, proj, conv_w, conv_w, conv_w, conv_b2, conv_b2, conv_b2,
      cm_bf16, sm_bf16, kr, ki, kn, filt_bias.reshape(HYENA_ORDER, 1, hw))


def _outproj_kernel(*refs, n_act, has_bias):
    acts = refs[:n_act]
    (w_ref, x_ref, g1_ref, ng_ref, sh_ref, sc_ref, rt_ref) = refs[n_act:n_act + 7]
    rest = refs[n_act + 7:]
    x1_ref, h2_ref, lg_ref = rest[-3:]
    y = None
    k0 = 0
    for a_ref in acts:
        k = a_ref.shape[-1]
        part = jnp.dot(a_ref[...].astype(BF16), w_ref[k0:k0 + k, :], preferred_element_type=F32)
        y = part if y is None else y + part
        k0 += k
    if has_bias:
        y = y + rest[0][...]
    x1 = x_ref[...] + g1_ref[...] * y
    x1_ref[...] = x1
    h2 = _norm_mod(x1, ng_ref[...], sh_ref[...], sc_ref[...])
    h_hi = h2.astype(BF16)
    h_lo = (h2 - h_hi.astype(F32)).astype(BF16)
    h2_ref[...] = h2
    ne = lg_ref.shape[0]
    nt = (((1,), (1,)), ((), ()))
    p_hi = lax.dot_general(rt_ref[...], h_hi, nt, preferred_element_type=F32)
    p_lo = lax.dot_general(rt_ref[0:ne, :], h_lo, nt, preferred_element_type=F32)
    lg_ref[...] = p_hi[0:ne, :] + p_hi[ne:2 * ne, :] + p_lo


def _outproj(acts, w_bf16, bias, x, mods, norm_g, router, tn=512):
    b, n, d = x.shape
    e = router.shape[1]
    tn = min(tn, n)
    row = lambda j: pl.BlockSpec((None, 1, d), lambda i, t: (i * N_MOD + j, 0, 0))
    in_specs = [pl.BlockSpec((None, tn, a.shape[-1]), lambda i, t: (i, t, 0)) for a in acts]
    in_specs += [_single(w_bf16.shape, lambda i, t: (0, 0)),
                 pl.BlockSpec((None, tn, d), lambda i, t: (i, t, 0)),
                 row(2), pl.BlockSpec((1, d), lambda i, t: (0, 0)), row(3), row(4),
                 pl.BlockSpec((2 * e, d), lambda i, t: (0, 0))]
    rt = router.T
    rt_hi = rt.astype(BF16)
    rt_lo = (rt - rt_hi.astype(F32)).astype(BF16)
    args = list(acts) + [w_bf16, x, mods, norm_g.reshape(1, d), mods, mods,
                         jnp.concatenate([rt_hi, rt_lo], axis=0)]
    if bias is not None:
        in_specs.append(pl.BlockSpec((1, d), lambda i, t: (0, 0)))
        args.append(bias.reshape(1, d))
    return pl.pallas_call(
        functools.partial(_outproj_kernel, n_act=len(acts), has_bias=bias is not None),
        grid=(b, n // tn),
        in_specs=in_specs,
        out_specs=[pl.BlockSpec((None, tn, d), lambda i, t: (i, t, 0)),
                   pl.BlockSpec((None, tn, d), lambda i, t: (i, t, 0)),
                   pl.BlockSpec((None, e, tn), lambda i, t: (i, 0, t))],
        out_shape=[jax.ShapeDtypeStruct((b, n, d), F32),
                   jax.ShapeDtypeStruct((b, n, d), F32),
                   jax.ShapeDtypeStruct((b, e, n), F32)],
        compiler_params=_params(2),
        name="outproj_router",
    )(*args)


def _cumsum_lanes(x):
    n = x.shape[-1]
    lane = lax.broadcasted_iota(jnp.int32, x.shape, 1)
    s = 1
    while s < n:
        x = x + jnp.where(lane >= s, pltpu.roll(x, s, 1), 0)
        s *= 2
    return x


def _compact_lanes(slot, sel, payloads):
    n = slot.shape[-1]
    lane = lax.broadcasted_iota(jnp.int32, slot.shape, 1)
    enc = jnp.where(sel, ((lane - slot) << 1) | 1, 0)
    j = 0
    while (1 << j) < n:
        sh = 1 << j
        inside = lane < n - sh
        src_enc = jnp.where(inside, pltpu.roll(enc, n - sh, 1), 0)
        take = ((src_enc >> (j + 1)) & 1) > 0
        stay = (enc & (1 - ((enc >> (j + 1)) & 1))) > 0
        payloads = [jnp.where(take, pltpu.roll(v, n - sh, 1), v) for v in payloads]
        enc = jnp.where(take, src_enc, jnp.where(stay, enc, 0))
        j += 1
    return payloads


def _route_kernel(lg_ref, idx_ref, gslot_ref, *, cap):
    for s in range(lg_ref.shape[0]):
        _route_one(lg_ref.at[s], idx_ref.at[s], gslot_ref.at[s], cap)


def _route_one(lg_ref, idx_ref, gslot_ref, cap):
    lg = lg_ref[...]
    m = jnp.max(lg, axis=0, keepdims=True)
    ex = jnp.exp(lg - m)
    probs = ex / jnp.sum(ex, axis=0, keepdims=True)
    thr = jnp.zeros((lg.shape[0], 1), jnp.int32)
    for bit in range(30, -1, -1):
        cand = thr | (1 << bit)
        cnt = jnp.sum(jnp.where(probs >= pltpu.bitcast(cand, F32), 1.0, 0.0), axis=1, keepdims=True)
        thr = jnp.where(cnt >= cap, cand, thr)
    gt = probs >= pltpu.bitcast(thr + 1, F32)
    eq = jnp.logical_and(probs >= pltpu.bitcast(thr, F32), jnp.logical_not(gt))
    need = cap - jnp.sum(jnp.where(gt, 1.0, 0.0), axis=1, keepdims=True)
    eq_rank = _cumsum_lanes(jnp.where(eq, 1.0, 0.0))
    sel = jnp.where(gt, 1.0, jnp.where(eq, jnp.where(eq_rank <= need, 1.0, 0.0), 0.0))
    slot = (_cumsum_lanes(sel) - 1.0).astype(jnp.int32)
    lane = lax.broadcasted_iota(jnp.int32, slot.shape, 1)
    idx, gslot = _compact_lanes(slot, sel > 0.0, [lane, probs])
    idx_ref[...] = idx[:, :idx_ref.shape[-1]]
    gslot_ref[...] = gslot[:, :gslot_ref.shape[-1]]


ROUTE_SAMPLES = 4


def _route(logits, cap):
    b, e, n = logits.shape
    g = math.gcd(ROUTE_SAMPLES, b)
    capw = -(-cap // LANES) * LANES
    slots = pl.BlockSpec((g, e, capw), lambda i: (i, 0, 0))
    return pl.pallas_call(
        functools.partial(_route_kernel, cap=cap),
        grid=(b // g,),
        in_specs=[pl.BlockSpec((g, e, n), lambda i: (i, 0, 0))],
        out_specs=[slots, slots],
        out_shape=[jax.ShapeDtypeStruct((b, e, capw), jnp.int32),
                   jax.ShapeDtypeStruct((b, e, capw), F32)],
        compiler_params=_params(1),
        name="route",
    )(logits)


MOE_ROW_CHUNK = 512
SCATTER_GROUP = 4


def _moe_kernel(idx_ref, gs_ref, h_ref, x_ref, g2_ref, w1_ref, w3_ref, w2_ref, *rest, cap, final):
    n_extra = 1 if final else 0
    o_ref, xs_s, ys_s = rest[n_extra:n_extra + 3]
    e = pl.program_id(1)
    ne = pl.num_programs(1)
    n = h_ref.shape[0]
    tr = min(MOE_ROW_CHUNK, n)

    def gather_rows(k):
        for c in range(cap):
            xs_s[c:c + 1, :] = h_ref[pl.ds(idx_ref[k, c], 1), :]

    def scatter_rows(k, live):
        for c0 in range(0, cap, SCATTER_GROUP):
            rows = [pl.ds(idx_ref[k, c], 1) for c in range(c0, c0 + SCATTER_GROUP)]
            new = [o_ref[row, :] + (gs_ref[k, c0 + j] * live) * ys_s[c0 + j:c0 + j + 1, :]
                   for j, row in enumerate(rows)]
            for row, val in zip(rows, new):
                o_ref[row, :] = val

    @pl.when(e == 0)
    def _():
        o_ref[...] = x_ref[...]
        ys_s[...] = jnp.zeros_like(ys_s)
        gather_rows(0)

    xs = xs_s[...].astype(BF16)
    gather_rows(jnp.minimum(e + 1, ne - 1))
    scatter_rows(jnp.maximum(e - 1, 0), jnp.where(e > 0, 1.0, 0.0))
    a = jnp.dot(xs, w1_ref[...], preferred_element_type=F32)
    g = jnp.dot(xs, w3_ref[...], preferred_element_type=F32)
    y = jnp.dot((_silu(a) * g).astype(BF16), w2_ref[...], preferred_element_type=F32)
    ys_s[...] = y * g2_ref[...]

    @pl.when(e == ne - 1)
    def _():
        scatter_rows(ne - 1, 1.0)

    if final:
        @pl.when(e == ne - 1)
        def _():
            def norm_rows(ci, carry):
                rows = pl.ds(pl.multiple_of(ci * tr, tr), tr)
                xf = o_ref[rows, :]
                ms = jnp.mean(xf * xf, axis=-1, keepdims=True)
                o_ref[rows, :] = xf * lax.rsqrt(ms + EPS) * rest[0][...]
                return carry
            lax.fori_loop(0, n // tr, norm_rows, 0)


def _cast_kernel(*refs):
    k = len(refs) // 2
    for x_ref, o_ref in zip(refs[:k], refs[k:]):
        o_ref[...] = x_ref[...].astype(o_ref.dtype)


def _cast_bf16(*ws):
    g = ws[0].shape[0]
    specs = [pl.BlockSpec((None,) + w.shape[1:], lambda i: (i, 0, 0)) for w in ws]
    return pl.pallas_call(
        _cast_kernel,
        grid=(g,),
        in_specs=specs,
        out_specs=specs,
        out_shape=[jax.ShapeDtypeStruct(w.shape, BF16) for w in ws],
        compiler_params=_params(1),
        name="cast_bf16",
    )(*ws)


def _moe(h2, idx, gslot, x1, mods, w1, w3, w2, layer, final_g=None):
    b, n, d = x1.shape
    e = idx.shape[1]
    f = w1.shape[-1]
    cap = EC_CAPACITY * n // e
    final = final_g is not None
    table = pl.BlockSpec((None, e, idx.shape[-1]), lambda i, k: (i, 0, 0), memory_space=pltpu.SMEM)
    in_specs = [table, table,
                pl.BlockSpec((None, n, d), lambda i, k: (i, 0, 0)),
                _single((None, n, d), lambda i, k: (i, 0, 0)),
                pl.BlockSpec((None, 1, d), lambda i, k: (i * N_MOD + 5, 0, 0)),
                pl.BlockSpec((None, d, f), lambda i, k: (layer * e + k, 0, 0)),
                pl.BlockSpec((None, d, f), lambda i, k: (layer * e + k, 0, 0)),
                pl.BlockSpec((None, f, d), lambda i, k: (layer * e + k, 0, 0))]
    args = [idx, gslot, h2, x1, mods, w1, w3, w2]
    if final:
        in_specs.append(pl.BlockSpec((1, d), lambda i, k: (0, 0)))
        args.append(final_g.reshape(1, d))
    return pl.pallas_call(
        functools.partial(_moe_kernel, cap=cap, final=final),
        grid=(b, e),
        in_specs=in_specs,
        out_specs=pl.BlockSpec((None, n, d), lambda i, k: (i, 0, 0)),
        out_shape=jax.ShapeDtypeStruct((b, n, d), F32),
        scratch_shapes=[pltpu.VMEM((cap, d), F32), pltpu.VMEM((cap, d), F32)],
        compiler_params=_params(2),
        name="moe",
    )(*args)


VCONV_GROUP = 4


def _vconv_kernel(u_ref, w_ref, b_ref, lg_ref, lb_ref, o_ref, pad_s, y_s, *, n, k_taps, group):
    d = u_ref.shape[-1]
    groups = d // LANES
    n_rows = n // GRID_W
    top = (k_taps - 1) // 2
    halo = (group - 1) * GRID_W
    span = group * GRID_W
    tcp = min(256, n)

    for j in range(groups):
        if halo:
            pad_s[j, 0:halo, :] = jnp.zeros((halo, LANES), F32)
            pad_s[j, halo + n:halo + n + halo, :] = jnp.zeros((halo, LANES), F32)

    def fill(ci, carry):
        s = pl.multiple_of(ci * tcp, tcp)
        x = u_ref[pl.ds(s, tcp), :].astype(F32)
        for j in range(groups):
            pad_s[j, pl.ds(halo + s, tcp), :] = x[:, j * LANES:(j + 1) * LANES]
        return carry

    lax.fori_loop(0, n // tcp, fill, 0)

    for g in range(n_rows // group):
        r0 = g * group
        taps = [k for k in range(k_taps) if any(0 <= r0 + j + k - top < n_rows for j in range(group))]

        def lane_group(cj, carry, r0=r0, taps=taps):
            acc = None
            for k in taps:
                s = (r0 + k - top) * GRID_W + halo
                term = pad_s[cj, s:s + span, :] * w_ref[cj, k:k + 1, :]
                acc = term if acc is None else acc + term
            y_s[cj] = acc + b_ref[cj]
            return carry

        lax.fori_loop(0, groups, lane_group, 0)
        for j in range(group):
            acc = jnp.concatenate([y_s[cj, j * GRID_W:(j + 1) * GRID_W, :] for cj in range(groups)], axis=1)
            mu = jnp.mean(acc, axis=-1, keepdims=True)
            xc = acc - mu
            var = jnp.mean(xc * xc, axis=-1, keepdims=True)
            y = xc * lax.rsqrt(var + EPS) * lg_ref[...] + lb_ref[...]
            o_ref[(r0 + j) * GRID_W:(r0 + j + 1) * GRID_W, :] = _silu(y).astype(o_ref.dtype)


def _vconv_ln_silu(u, dw_w, dw_b, ln_g, ln_b):
    b, n, d = u.shape
    k_taps = dw_w.shape[0]
    groups = d // LANES
    group = math.gcd(VCONV_GROUP, n // GRID_W)
    halo = (group - 1) * GRID_W
    full = lambda shape: pl.BlockSpec(shape, lambda i: tuple(0 for _ in shape))
    w3 = dw_w.reshape(k_taps, groups, LANES).transpose(1, 0, 2)
    b3 = dw_b.reshape(groups, 1, LANES)
    return pl.pallas_call(
        functools.partial(_vconv_kernel, n=n, k_taps=k_taps, group=group),
        grid=(b,),
        in_specs=[pl.BlockSpec((None, n, d), lambda i: (i, 0, 0)),
                  full((groups, k_taps, LANES)), full((groups, 1, LANES)), full((1, d)), full((1, d))],
        out_specs=pl.BlockSpec((None, n, d), lambda i: (i, 0, 0)),
        out_shape=jax.ShapeDtypeStruct((b, n, d), BF16),
        scratch_shapes=[pltpu.VMEM((groups, n + 2 * halo, LANES), F32),
                        pltpu.VMEM((groups, group * GRID_W, LANES), F32)],
        compiler_params=_params(1),
        name="vconv_ln_silu",
    )(u, w3, b3, ln_g.reshape(1, d), ln_b.reshape(1, d))


def kernel(x, c, ctx, c_ctx, norm1_g, norm2_g, w_mod, b_mod, w_in, conv_a_w, conv_a_b, lru_wr, lru_br, lru_wi, lru_bi, lru_lam, conv_b_w, conv_b_b, filt_w1, filt_b1, filt_w2, filt_b2, filt_w3, filt_b3, filt_freq, filt_bias, w_out, cf_w1, cf_b1, cf_dw_w, cf_dw_b, cf_ln_g, cf_ln_b, cf_w2, cf_b2, router, exp_w1, exp_w3, exp_w2, final_g):
    bsz, n, d = x.shape
    depth = w_mod.shape[0]
    assert depth == 2 and w_in.shape[0] == 1 and cf_w1.shape[0] == 1
    rw = conv_a_w.shape[-1]
    hw = filt_bias.shape[-1]
    e = router.shape[-1]
    cap = EC_CAPACITY * n // e

    rows = -(-(bsz + 1) // SUBLANES) * SUBLANES
    cvec = jnp.concatenate([c, c_ctx[None], jnp.zeros((rows - bsz - 1, d), F32)], axis=0)
    mods_all = _adaln(cvec, w_mod, b_mod)
    mods = [mods_all[l].reshape(rows * N_MOD, 1, d) for l in range(depth)]

    w_in0 = w_in[0].astype(BF16)
    wg, bg = _lru_gate_weights(lru_wr[0], lru_wi[0], lru_br[0], lru_bi[0])
    ctx_xa = _norm_matmul(ctx, norm1_g[0], mods[0], lambda i: bsz, 0, 1, w_in0[:, rw:2 * rw],
                          out_dtype=BF16)
    h0_ctx = jnp.zeros((bsz, 2, rw), F32)
    _, ctx_state = _rglru(ctx_xa, 0, None, 0, conv_a_w[0], conv_a_b[0], wg, bg, lru_lam[0], h0_ctx)
    proj = _norm_matmul(x, norm1_g[0], mods[0], lambda i: i, 0, 1, w_in0, out_dtype=BF16)
    y_a, _ = _rglru(proj, 1, proj, 0, conv_a_w[0], conv_a_b[0], wg, bg, lru_lam[0], ctx_state,
                    out_dtype=BF16)

    fwd, bwd = _hyena_filters(n, filt_w1[0], filt_b1[0], filt_w2[0], filt_b2[0],
                              filt_w3[0], filt_b3[0], filt_freq[0])
    kpos, kneg = _polyphase_filters(fwd, bwd)
    cm, sm = (jnp.asarray(t) for t in _dft_tables(n // HYENA_PHASES))
    cm_hi, sm_hi = cm.astype(BF16), sm.astype(BF16)
    cm_lo = (cm - cm_hi.astype(F32)).astype(BF16)
    sm_lo = (sm - sm_hi.astype(F32)).astype(BF16)
    kr, ki, kn = _hyena_spectrum(cm_hi, cm_lo, sm_hi, sm_lo, kpos, kneg)
    z = _hyena(proj, 2 * rw, conv_b_w[0], conv_b_b[0], cm_hi, sm_hi,
               kr, ki, kn, filt_bias[0], out_dtype=BF16)

    ff = exp_w1.shape[-1]
    ew1, ew3, ew2 = _cast_bf16(exp_w1.reshape(depth * e, d, ff), exp_w3.reshape(depth * e, d, ff),
                               exp_w2.reshape(depth * e, ff, d))

    x1, h2, logits = _outproj([y_a, z], w_out[0].astype(BF16), None, x, mods[0], norm2_g[0], router[0])
    idx, gslot = _route(logits, cap)
    xl = _moe(h2, idx, gslot, x1, mods[0], ew1, ew3, ew2, 0)

    u = _norm_matmul(xl, norm1_g[1], mods[1], lambda i: i, 0, 1, cf_w1[0].astype(BF16),
                     bias=cf_b1[0], glu=True, out_dtype=BF16)
    v = _vconv_ln_silu(u, cf_dw_w[0], cf_dw_b[0], cf_ln_g[0], cf_ln_b[0])
    x1, h2, logits = _outproj([v], cf_w2[0].astype(BF16), cf_b2[0], xl, mods[1], norm2_g[1], router[1])
    idx, gslot = _route(logits, cap)
    return _moe(h2, idx, gslot, x1, mods[1], ew1, ew3, ew2, 1, final_g=final_g)
```

```python
import functools
import math

import numpy as np
import jax
import jax.numpy as jnp
from jax import lax
from jax.experimental import pallas as pl
from jax.experimental.pallas import tpu as pltpu

F32 = jnp.float32
BF16 = jnp.bfloat16
HIGHEST = lax.Precision.HIGHEST

EPS = 1e-6
GRID_W = 64
N_MOD = 6
LRU_HEADS = 8
LRU_C = 8.0
HYENA_ORDER = 2
HYENA_DECAY_TARGET = 1e-2
HYENA_FAST_PCT = 0.3
HYENA_SLOW_PCT = 1.5
HYENA_SHIFT = 0.05
EC_CAPACITY = 2

V7X_VMEM_BYTES = 64 * 1024 * 1024
VMEM_LIMIT = V7X_VMEM_BYTES - 4 * 1024 * 1024
SUBLANES = 8
LANES = 128


def _params(n_axes):
    return pltpu.CompilerParams(dimension_semantics=("arbitrary",) * n_axes,
                                vmem_limit_bytes=VMEM_LIMIT)


def _single(block_shape, index_map):
    return pl.BlockSpec(block_shape, index_map, pipeline_mode=pl.Buffered(1))


def _silu(x):
    return x * jax.nn.sigmoid(x)


def _split_hi_lo(x):
    hi = x.astype(BF16)
    lo = (x - hi.astype(F32)).astype(BF16)
    return jnp.concatenate([hi, lo], axis=-1)


def _norm_mod(xf, g, shift, scale):
    ms = jnp.mean(xf * xf, axis=-1, keepdims=True)
    y = xf * lax.rsqrt(ms + EPS) * g
    return y * (1.0 + scale) + shift


def _adaln_kernel(c_ref, w_ref, b_ref, o_ref):
    s = _silu(c_ref[...])
    o_ref[...] = jnp.dot(s, w_ref[...], precision=HIGHEST, preferred_element_type=F32) + b_ref[...]


def _adaln(cvec, w_mod, b_mod, tc=512):
    depth, d, dm = w_mod.shape
    r = cvec.shape[0]
    return pl.pallas_call(
        _adaln_kernel,
        grid=(depth, dm // tc),
        in_specs=[pl.BlockSpec((r, d), lambda l, j: (0, 0)),
                  pl.BlockSpec((None, d, tc), lambda l, j: (l, 0, j)),
                  pl.BlockSpec((None, 1, tc), lambda l, j: (l, 0, j))],
        out_specs=pl.BlockSpec((None, r, tc), lambda l, j: (l, 0, j)),
        out_shape=jax.ShapeDtypeStruct((depth, r, dm), F32),
        compiler_params=_params(2),
        name="adaln",
    )(cvec, w_mod, b_mod.reshape(depth, 1, dm))


def _norm_matmul_kernel(x_ref, g_ref, sh_ref, sc_ref, w_ref, *rest, has_bias, glu):
    o_ref = rest[-1]
    h = _norm_mod(x_ref[...], g_ref[...], sh_ref[...], sc_ref[...])
    y = jnp.dot(h.astype(BF16), w_ref[...], preferred_element_type=F32)
    if has_bias:
        y = y + rest[0][...]
    if glu:
        half = y.shape[-1] // 2
        y = y[:, :half] * jax.nn.sigmoid(y[:, half:])
    o_ref[...] = y.astype(o_ref.dtype)


def _norm_matmul(x, g, mods, mod_row, shift_j, scale_j, w_bf16, bias=None, glu=False, tn=512,
                 out_dtype=F32):
    b, n, d = x.shape
    nout = w_bf16.shape[1]
    tn = min(tn, n)
    out_w = nout // 2 if glu else nout
    in_specs = [pl.BlockSpec((None, tn, d), lambda i, j: (i, j, 0)),
                pl.BlockSpec((1, d), lambda i, j: (0, 0)),
                pl.BlockSpec((None, 1, d), lambda i, j: (mod_row(i) * N_MOD + shift_j, 0, 0)),
                pl.BlockSpec((None, 1, d), lambda i, j: (mod_row(i) * N_MOD + scale_j, 0, 0)),
                _single((d, nout), lambda i, j: (0, 0))]
    args = [x, g.reshape(1, d), mods, mods, w_bf16]
    if bias is not None:
        in_specs.append(pl.BlockSpec((1, nout), lambda i, j: (0, 0)))
        args.append(bias.reshape(1, nout))
    return pl.pallas_call(
        functools.partial(_norm_matmul_kernel, has_bias=bias is not None, glu=glu),
        grid=(b, n // tn),
        in_specs=in_specs,
        out_specs=pl.BlockSpec((None, tn, out_w), lambda i, j: (i, j, 0)),
        out_shape=jax.ShapeDtypeStruct((b, n, out_w), out_dtype),
        compiler_params=_params(2),
        name="norm_matmul",
    )(*args)


HALO = 16


def _shift_rows(cur, k, prev, nxt, has_prev, has_next):
    t = cur.shape[0]
    row = lax.broadcasted_iota(jnp.int32, cur.shape, 0)
    if k < 0:
        out = pltpu.roll(cur, -k, 0)
        for r in range(-k):
            src = prev[HALO + k + r:HALO + k + r + 1, :]
            fill = jnp.where(has_prev, src, 0.0)
            out = jnp.where(row == r, fill, out)
    else:
        out = pltpu.roll(cur, t - k, 0)
        for r in range(k):
            src = nxt[r:r + 1, :]
            fill = jnp.where(has_next, src, 0.0)
            out = jnp.where(row == t - k + r, fill, out)
    return out


def _dwconv_chunk(x_ref, s, t, n, w_ref, b_ref, pad_left):
    cur = x_ref[pl.ds(s, t), :].astype(F32)
    prev = x_ref[pl.ds(pl.multiple_of(jnp.maximum(s - HALO, 0), HALO), HALO), :].astype(F32)
    nxt = x_ref[pl.ds(pl.multiple_of(jnp.minimum(s + t, n - HALO), HALO), HALO), :].astype(F32)
    has_prev = s > 0
    has_next = s + t < n
    k_taps = w_ref.shape[0]
    acc = None
    for k in range(k_taps):
        off = k - pad_left
        xs = cur if off == 0 else _shift_rows(cur, off, prev, nxt, has_prev, has_next)
        term = xs * w_ref[k:k + 1, :]
        acc = term if acc is None else acc + term
    return acc + b_ref[...]


GATE_CHUNK = 256
SCAN_CHUNK = 64


def _scan_rows(a, b, carry, reverse):
    t = a.shape[0]
    row = lax.broadcasted_iota(jnp.int32, a.shape, 0)
    s = 1
    while s < t:
        if reverse:
            valid = row < t - s
            sh = t - s
        else:
            valid = row >= s
            sh = s
        a_s = jnp.where(valid, pltpu.roll(a, sh, 0), 1.0)
        b_s = jnp.where(valid, pltpu.roll(b, sh, 0), 0.0)
        b = a * b_s + b
        a = a * a_s
        s *= 2
    h = b + a * carry
    return h, (h[0:1] if reverse else h[t - 1:t])


def _gelu_tanh(x):
    return 0.5 * x * (1.0 + jnp.tanh(math.sqrt(2.0 / math.pi) * (x + 0.044715 * (x * x * x))))


def _rglru_kernel(*refs, n, with_gate):
    if with_gate:
        (xa_ref, gate_ref, cw_ref, cb_ref, wg_ref, bg_ref, lam_ref, h0_ref,
         y_ref, hl_ref, af_s, uf_s, ab_s, ub_s) = refs
    else:
        (xa_ref, cw_ref, cb_ref, wg_ref, bg_ref, lam_ref, h0_ref,
         hl_ref, af_s, uf_s, ab_s, ub_s) = refs
    w = xa_ref.shape[-1]
    tg = min(GATE_CHUNK, n)
    ts = min(SCAN_CHUNK, n)
    lam = lam_ref[...]
    sp = jnp.maximum(-lam, 0.0) + jnp.log(1.0 + jnp.exp(-jnp.abs(lam)))
    half_c_sp = (-0.5 * LRU_C) * sp

    def gates(ci, carry):
        s = pl.multiple_of(ci * tg, tg)
        xa = _dwconv_chunk(xa_ref, s, tg, n, cw_ref, cb_ref, 1)
        z = jnp.dot(xa.astype(BF16), wg_ref[...], preferred_element_type=F32) + bg_ref[...]
        xh = 0.5 * xa
        for d, (a_s, u_s) in enumerate(((af_s, uf_s), (ab_s, ub_s))):
            tr = jnp.tanh(0.5 * z[:, (2 * d) * w:(2 * d + 1) * w])
            ti = jnp.tanh(0.5 * z[:, (2 * d + 1) * w:(2 * d + 2) * w])
            c1 = half_c_sp[d:d + 1, :]
            a = jnp.exp(c1 * tr + c1)
            t = 1.0 - a * a
            root = jnp.where(t > 0.0, t * lax.rsqrt(t), 0.0)
            a_s[pl.ds(s, tg), :] = a
            u_s[pl.ds(s, tg), :] = root * (ti * xh + xh)
        return carry

    lax.fori_loop(0, n // tg, gates, 0)

    nc = n // ts

    def fwd(ci, carry):
        s = pl.multiple_of(ci * ts, ts)
        h, carry = _scan_rows(af_s[pl.ds(s, ts), :], uf_s[pl.ds(s, ts), :], carry, False)
        uf_s[pl.ds(s, ts), :] = h
        return carry

    hf_last = lax.fori_loop(0, nc, fwd, h0_ref[0:1, :])

    def bwd(ci, carry):
        s = pl.multiple_of((nc - 1 - ci) * ts, ts)
        h, carry = _scan_rows(ab_s[pl.ds(s, ts), :], ub_s[pl.ds(s, ts), :], carry, True)
        if with_gate:
            hsum = uf_s[pl.ds(s, ts), :] + h
            gate = gate_ref[pl.ds(s, ts), :].astype(F32)
            y_ref[pl.ds(s, ts), :] = (_gelu_tanh(gate) * hsum).astype(y_ref.dtype)
        return carry

    hb_last = lax.fori_loop(0, nc, bwd, h0_ref[1:2, :])
    hl_ref[0:1, :] = hf_last
    hl_ref[1:2, :] = hb_last


def _rglru(xa_src, xa_col, gate_src, gate_col, conv_w, conv_b, wg_bf16, bg, lam, h0, out_dtype=F32):
    b, n, _ = xa_src.shape
    w = conv_w.shape[1]
    with_gate = gate_src is not None
    seq = lambda col: pl.BlockSpec((None, n, w), lambda i: (i, 0, col))
    full = lambda shape: pl.BlockSpec(shape, lambda i: tuple(0 for _ in shape))
    in_specs = [seq(xa_col)]
    args = [xa_src]
    if with_gate:
        in_specs.append(seq(gate_col))
        args.append(gate_src)
    in_specs += [full(conv_w.shape), full((1, w)), _single(wg_bf16.shape, lambda i: (0, 0)),
                 full((1, 4 * w)), full((2, w)), pl.BlockSpec((None, 2, w), lambda i: (i, 0, 0))]
    args += [conv_w, conv_b.reshape(1, w), wg_bf16, bg.reshape(1, 4 * w), lam, h0]
    out_specs = [pl.BlockSpec((None, 2, w), lambda i: (i, 0, 0))]
    out_shape = [jax.ShapeDtypeStruct((b, 2, w), F32)]
    if with_gate:
        out_specs.insert(0, pl.BlockSpec((None, n, w), lambda i: (i, 0, 0)))
        out_shape.insert(0, jax.ShapeDtypeStruct((b, n, w), out_dtype))
    outs = pl.pallas_call(
        functools.partial(_rglru_kernel, n=n, with_gate=with_gate),
        grid=(b,),
        in_specs=in_specs,
        out_specs=out_specs,
        out_shape=out_shape,
        scratch_shapes=[pltpu.VMEM((n, w), F32)] * 4,
        compiler_params=_params(1),
        name="rglru_gated" if with_gate else "rglru_state",
    )(*args)
    return (outs[0], outs[1]) if with_gate else (None, outs[0])


def _lru_gate_weights(lru_wr, lru_wi, lru_br, lru_bi):
    def dense(wh):
        heads, hd, _ = wh.shape
        eye = jnp.eye(heads, dtype=wh.dtype)
        return jnp.einsum('hij,hg->higj', wh, eye).reshape(heads * hd, heads * hd)
    wg = jnp.concatenate([dense(lru_wr[0]), dense(lru_wi[0]), dense(lru_wr[1]), dense(lru_wi[1])], axis=1)
    bg = jnp.concatenate([lru_br[0], lru_bi[0], lru_br[1], lru_bi[1]], axis=0)
    return wg.astype(BF16), bg


@functools.lru_cache(maxsize=None)
def _dft_tables(n):
    m = 2 * n
    k = (np.arange(n, dtype=np.int64)[:, None] * np.arange(n, dtype=np.int64)[None, :]) % m
    ang = 2.0 * np.pi * np.arange(m, dtype=np.float64) / m
    cm = np.cos(ang)[k].astype(np.float32)
    sm = np.sin(ang)[k].astype(np.float32)
    return cm, sm


@functools.lru_cache(maxsize=None)
def _filter_features(n, emb):
    t = np.linspace(0.0, 1.0, n, dtype=np.float32)[:, None]
    bands = (emb - 1) // 2
    w = (np.float32(2.0 * math.pi / n) * np.arange(n, dtype=np.float32))[:, None]
    f = np.linspace(1e-4, bands - 1, bands, dtype=np.float32)[None, :]
    z = np.concatenate([t, np.cos(f * w), -np.sin(f * w)], axis=-1).astype(np.float32)
    return z


@functools.lru_cache(maxsize=None)
def _hyena_deltas(width):
    d = np.abs(np.linspace(math.log(HYENA_DECAY_TARGET) / HYENA_SLOW_PCT,
                           math.log(HYENA_DECAY_TARGET) / HYENA_FAST_PCT, width, dtype=np.float32))
    return d.reshape(1, width).astype(np.float32)


def _filter_kernel(z_ref, t_ref, w1_ref, b1_ref, w2_ref, b2_ref, w3_ref, b3_ref, fr_ref, dl_ref,
                   fwd_ref, bwd_ref):
    hw = dl_ref.shape[-1]
    hid = jnp.sin(fr_ref[0:1, :] * (jnp.dot(z_ref[...], w1_ref[...], precision=HIGHEST,
                                            preferred_element_type=F32) + b1_ref[...]))
    hid = jnp.sin(fr_ref[1:2, :] * (jnp.dot(hid, w2_ref[...], precision=HIGHEST,
                                            preferred_element_type=F32) + b2_ref[...]))
    filt = jnp.dot(hid, w3_ref[...], precision=HIGHEST, preferred_element_type=F32) + b3_ref[...]
    window = jnp.exp(-t_ref[...] * dl_ref[...]) + HYENA_SHIFT
    for o in range(HYENA_ORDER):
        fwd_ref[o] = filt[:, (2 * o) * hw:(2 * o + 1) * hw] * window
        bwd_ref[o] = filt[:, (2 * o + 1) * hw:(2 * o + 2) * hw] * window


def _hyena_filters(n, filt_w1, filt_b1, filt_w2, filt_b2, filt_w3, filt_b3, filt_freq, tn=256):
    emb, hidden = filt_w1.shape
    hw = filt_w3.shape[1] // (2 * HYENA_ORDER)
    tn = min(tn, n)
    z = jnp.asarray(_filter_features(n, emb))
    t = jnp.asarray(np.linspace(0.0, 1.0, n, dtype=np.float32)[:, None])
    dl = jnp.asarray(_hyena_deltas(hw))
    full = lambda shape: pl.BlockSpec(shape, lambda j: tuple(0 for _ in shape))
    return pl.pallas_call(
        _filter_kernel,
        grid=(n // tn,),
        in_specs=[pl.BlockSpec((tn, emb), lambda j: (j, 0)),
                  pl.BlockSpec((tn, 1), lambda j: (j, 0)),
                  full((emb, hidden)), full((1, hidden)), full((hidden, hidden)), full((1, hidden)),
                  full((hidden, 2 * HYENA_ORDER * hw)), full((1, 2 * HYENA_ORDER * hw)),
                  full((2, hidden)), full((1, hw))],
        out_specs=[pl.BlockSpec((HYENA_ORDER, tn, hw), lambda j: (0, j, 0))] * 2,
        out_shape=[jax.ShapeDtypeStruct((HYENA_ORDER, n, hw), F32)] * 2,
        compiler_params=_params(1),
        name="hyena_filter",
    )(z, t, filt_w1, filt_b1.reshape(1, -1), filt_w2, filt_b2.reshape(1, -1),
      filt_w3, filt_b3.reshape(1, -1), filt_freq, dl)


def _dot_hi_lo(t_hi, t_lo, f_cat):
    hw = f_cat.shape[-1] // 2
    p = jnp.dot(t_hi, f_cat, preferred_element_type=F32)
    return p[:, :hw] + p[:, hw:] + jnp.dot(t_lo, f_cat[:, :hw], preferred_element_type=F32)


HYENA_PHASES = 4


def _polyphase_filters(fwd, bwd):
    r = HYENA_PHASES
    order, n, hw = fwd.shape
    fp = fwd.reshape(order, n // r, r, hw)
    bp = bwd.reshape(order, n // r, r, hw)

    def shift1(a, first):
        return jnp.concatenate([first[:, None], a[:, :-1]], axis=1)

    def no_lag0(a):
        return a.at[:, 0].set(0.0)

    zero = jnp.zeros((order, hw), fwd.dtype)
    pos, neg = [], []
    for d in range(-(r - 1), r):
        if d == 0:
            pos.append(fp[:, :, 0])
            neg.append(no_lag0(bp[:, :, 0]))
        elif d > 0:
            pos.append(fp[:, :, d])
            neg.append(shift1(bp[:, :, r - d], zero))
        else:
            pos.append(shift1(fp[:, :, r + d], bp[:, 0, -d]))
            neg.append(no_lag0(bp[:, :, -d]))
    return jnp.stack(pos, axis=1), jnp.stack(neg, axis=1)


def _spectrum_kernel(ch_ref, cl_ref, sh_ref, sl_ref, kp_ref, km_ref, kr_ref, ki_ref, kn_ref):
    m = kp_ref.shape[0]
    ksum = kp_ref[...] + km_ref[...]
    kdif = km_ref[...] - kp_ref[...]
    row = lax.broadcasted_iota(jnp.int32, ksum.shape, 0)
    scale = jnp.where(row == 0, 1.0 / (2 * m), 2.0 / (2 * m))
    kr_ref[...] = scale * _dot_hi_lo(ch_ref[...], cl_ref[...], _split_hi_lo(ksum))
    ki_ref[...] = scale * _dot_hi_lo(sh_ref[...], sl_ref[...], _split_hi_lo(kdif))
    alt = jnp.where((row & 1) == 0, 1.0, -1.0)
    kn_ref[...] = jnp.sum(ksum * alt, axis=0, keepdims=True) * (1.0 / (2 * m))


def _hyena_spectrum(c_hi, c_lo, s_hi, s_lo, kpos, kneg):
    order, nd, m, hw = kpos.shape
    tab = pl.BlockSpec((m, m), lambda o, d: (0, 0))
    filt = pl.BlockSpec((None, None, m, hw), lambda o, d: (o, d, 0, 0))
    row = pl.BlockSpec((None, None, 1, hw), lambda o, d: (o, d, 0, 0))
    return pl.pallas_call(
        _spectrum_kernel,
        grid=(order, nd),
        in_specs=[tab, tab, tab, tab, filt, filt],
        out_specs=[filt, filt, row],
        out_shape=[jax.ShapeDtypeStruct((order, nd, m, hw), F32)] * 2
        + [jax.ShapeDtypeStruct((order, nd, 1, hw), F32)],
        compiler_params=_params(2),
        name="hyena_spectrum",
    )(c_hi, c_lo, s_hi, s_lo, kpos, kneg)


HYENA_CHUNK = 512


def _hyena_kernel(v_ref, g0_ref, g1_ref, cwv_ref, cw0_ref, cw1_ref, cbv_ref, cb0_ref, cb1_ref,
                  cm_ref, sm_ref, kr_ref, ki_ref, kn_ref, fb_ref, z_ref,
                  u_s, g_s, x_s, *, n):
    r = HYENA_PHASES
    ct = v_ref.shape[-1]
    m = n // r
    tc = min(HYENA_CHUNK, n)
    row = lax.broadcasted_iota(jnp.int32, (m, ct), 0)
    alt = jnp.where((row & 1) == 0, 1.0, -1.0)
    rc = min(128, m)

    groups = ct // LANES

    def put_rows(dst_s, rows, val):
        for j in range(groups):
            dst_s[j, rows, :] = val[:, j * LANES:(j + 1) * LANES]

    def get_rows(src_s, rows):
        return jnp.concatenate([src_s[j, rows, :] for j in range(groups)], axis=1)

    def short_conv(src_ref, cw_ref, cb_ref, dst_s):
        def body(ci, carry):
            s = pl.multiple_of(ci * tc, tc)
            put_rows(dst_s, pl.ds(s, tc), _dwconv_chunk(src_ref, s, tc, n, cw_ref, cb_ref, 1))
            return carry
        lax.fori_loop(0, n // tc, body, 0)

    def phase(ref, p):
        return get_rows(ref, pl.ds(p, m, stride=r))

    short_conv(v_ref, cwv_ref, cbv_ref, u_s)
    for o, (g_ref, cw_ref, cb_ref) in enumerate(((g0_ref, cw0_ref, cb0_ref), (g1_ref, cw1_ref, cb1_ref))):
        short_conv(g_ref, cw_ref, cb_ref, g_s)
        nyq = []
        for q in range(r):
            uq = phase(u_s, q)
            ub = uq.astype(BF16)
            x_s[q, 0] = jnp.dot(cm_ref[...], ub, preferred_element_type=F32)
            x_s[q, 1] = jnp.dot(sm_ref[...], ub, preferred_element_type=F32)
            nyq.append(jnp.sum(uq * alt, axis=0, keepdims=True))
        for p in range(r):
            zc_parts, zs_parts = [], []
            for r0 in range(0, m, rc):
                zc = zs = None
                for q in range(r):
                    d = p - q + r - 1
                    xre = x_s[q, 0, r0:r0 + rc, :]
                    xim = x_s[q, 1, r0:r0 + rc, :]
                    kr = kr_ref[o, d, r0:r0 + rc, :]
                    ki = ki_ref[o, d, r0:r0 + rc, :]
                    tc_ = xre * kr + xim * ki
                    ts_ = xim * kr - xre * ki
                    zc = tc_ if zc is None else zc + tc_
                    zs = ts_ if zs is None else zs + ts_
                zc_parts.append(zc.astype(BF16))
                zs_parts.append(zs.astype(BF16))
            zc = jnp.concatenate(zc_parts, axis=0)
            zs = jnp.concatenate(zs_parts, axis=0)
            y = (jnp.dot(cm_ref[...], zc, preferred_element_type=F32)
                 + jnp.dot(sm_ref[...], zs, preferred_element_type=F32))
            ynyq = None
            for q in range(r):
                t = nyq[q] * kn_ref[o, p - q + r - 1]
                ynyq = t if ynyq is None else ynyq + t
            y = y + alt * ynyq + phase(u_s, p) * fb_ref[o]
            put_rows(u_s, pl.ds(p, m, stride=r), phase(g_s, p) * y)

    def emit(ci, carry):
        rows = pl.ds(pl.multiple_of(ci * tc, tc), tc)
        z_ref[rows, :] = get_rows(u_s, rows).astype(z_ref.dtype)
        return carry

    lax.fori_loop(0, n // tc, emit, 0)


def _hyena(proj, hy_col0, conv_w, conv_b, cm_bf16, sm_bf16, kr, ki, kn, filt_bias, ct=256,
           out_dtype=F32):
    b, n, _ = proj.shape
    hw = kr.shape[-1]
    nd, m = kr.shape[1], kr.shape[2]
    ct = min(ct, hw)
    nct = hw // ct
    base = hy_col0 // ct
    k_taps = conv_w.shape[0]

    def seq(j):
        return pl.BlockSpec((None, n, ct), lambda c, i: (i, 0, base + j * nct + c))

    def cw(j):
        return pl.BlockSpec((k_taps, ct), lambda c, i: (0, j * nct + c))

    def cb(j):
        return pl.BlockSpec((1, ct), lambda c, i: (0, j * nct + c))

    kspec = _single((HYENA_ORDER, nd, m, ct), lambda c, i: (0, 0, 0, c))
    knspec = pl.BlockSpec((HYENA_ORDER, nd, 1, ct), lambda c, i: (0, 0, 0, c))
    rowspec = pl.BlockSpec((HYENA_ORDER, 1, ct), lambda c, i: (0, 0, c))
    conv_b2 = conv_b.reshape(1, -1)
    return pl.pallas_call(
        functools.partial(_hyena_kernel, n=n),
        grid=(nct, b),
        in_specs=[seq(0), seq(1), seq(2), cw(0), cw(1), cw(2), cb(0), cb(1), cb(2),
                  pl.BlockSpec((m, m), lambda c, i: (0, 0)), pl.BlockSpec((m, m), lambda c, i: (0, 0)),
                  kspec, kspec, knspec, rowspec],
        out_specs=pl.BlockSpec((None, n, ct), lambda c, i: (i, 0, c)),
        out_shape=jax.ShapeDtypeStruct((b, n, hw), out_dtype),
        scratch_shapes=[pltpu.VMEM((ct // LANES, n, LANES), F32), pltpu.VMEM((ct // LANES, n, LANES), F32),
                        pltpu.VMEM((HYENA_PHASES, 2, m, ct), F32)],
        compiler_params=_params(2),
        name="hyena",
    )(proj, proj, proj, conv_w, conv_w, conv_w, conv_b2, conv_b2, conv_b2,
      cm_bf16, sm_bf16, kr, ki, kn, filt_bias.reshape(HYENA_ORDER, 1, hw))


def _outproj_kernel(*refs, n_act, has_bias):
    acts = refs[:n_act]
    (w_ref, x_ref, g1_ref, ng_ref, sh_ref, sc_ref, rt_ref) = refs[n_act:n_act + 7]
    rest = refs[n_act + 7:]
    x1_ref, h2_ref, lg_ref = rest[-3:]
    y = None
    k0 = 0
    for a_ref in acts:
        k = a_ref.shape[-1]
        part = jnp.dot(a_ref[...].astype(BF16), w_ref[k0:k0 + k, :], preferred_element_type=F32)
        y = part if y is None else y + part
        k0 += k
    if has_bias:
        y = y + rest[0][...]
    x1 = x_ref[...] + g1_ref[...] * y
    x1_ref[...] = x1
    h2 = _norm_mod(x1, ng_ref[...], sh_ref[...], sc_ref[...])
    h_hi = h2.astype(BF16)
    h_lo = (h2 - h_hi.astype(F32)).astype(BF16)
    half = h2.shape[-1] // 2
    h2_ref[...] = pltpu.pack_elementwise([h2[:, :half], h2[:, half:]], packed_dtype=BF16)
    ne = lg_ref.shape[0]
    nt = (((1,), (1,)), ((), ()))
    p_hi = lax.dot_general(rt_ref[...], h_hi, nt, preferred_element_type=F32)
    p_lo = lax.dot_general(rt_ref[0:ne, :], h_lo, nt, preferred_element_type=F32)
    lg_ref[...] = p_hi[0:ne, :] + p_hi[ne:2 * ne, :] + p_lo


def _outproj(acts, w_bf16, bias, x, mods, norm_g, router, tn=512):
    b, n, d = x.shape
    e = router.shape[1]
    tn = min(tn, n)
    row = lambda j: pl.BlockSpec((None, 1, d), lambda i, t: (i * N_MOD + j, 0, 0))
    in_specs = [pl.BlockSpec((None, tn, a.shape[-1]), lambda i, t: (i, t, 0)) for a in acts]
    in_specs += [_single(w_bf16.shape, lambda i, t: (0, 0)),
                 pl.BlockSpec((None, tn, d), lambda i, t: (i, t, 0)),
                 row(2), pl.BlockSpec((1, d), lambda i, t: (0, 0)), row(3), row(4),
                 pl.BlockSpec((2 * e, d), lambda i, t: (0, 0))]
    rt = router.T
    rt_hi = rt.astype(BF16)
    rt_lo = (rt - rt_hi.astype(F32)).astype(BF16)
    args = list(acts) + [w_bf16, x, mods, norm_g.reshape(1, d), mods, mods,
                         jnp.concatenate([rt_hi, rt_lo], axis=0)]
    if bias is not None:
        in_specs.append(pl.BlockSpec((1, d), lambda i, t: (0, 0)))
        args.append(bias.reshape(1, d))
    return pl.pallas_call(
        functools.partial(_outproj_kernel, n_act=len(acts), has_bias=bias is not None),
        grid=(b, n // tn),
        in_specs=in_specs,
        out_specs=[pl.BlockSpec((None, tn, d), lambda i, t: (i, t, 0)),
                   pl.BlockSpec((None, tn, d // 2), lambda i, t: (i, t, 0)),
                   pl.BlockSpec((None, e, tn), lambda i, t: (i, 0, t))],
        out_shape=[jax.ShapeDtypeStruct((b, n, d), F32),
                   jax.ShapeDtypeStruct((b, n, d // 2), jnp.uint32),
                   jax.ShapeDtypeStruct((b, e, n), F32)],
        compiler_params=_params(2),
        name="outproj_router",
    )(*args)


def _cumsum_lanes(x):
    n = x.shape[-1]
    lane = lax.broadcasted_iota(jnp.int32, x.shape, 1)
    s = 1
    while s < n:
        x = x + jnp.where(lane >= s, pltpu.roll(x, s, 1), 0)
        s *= 2
    return x


def _compact_lanes(slot, sel, payloads):
    n = slot.shape[-1]
    lane = lax.broadcasted_iota(jnp.int32, slot.shape, 1)
    enc = jnp.where(sel, ((lane - slot) << 1) | 1, 0)
    j = 0
    while (1 << j) < n:
        sh = 1 << j
        inside = lane < n - sh
        src_enc = jnp.where(inside, pltpu.roll(enc, n - sh, 1), 0)
        take = ((src_enc >> (j + 1)) & 1) > 0
        stay = (enc & (1 - ((enc >> (j + 1)) & 1))) > 0
        payloads = [jnp.where(take, pltpu.roll(v, n - sh, 1), v) for v in payloads]
        enc = jnp.where(take, src_enc, jnp.where(stay, enc, 0))
        j += 1
    return payloads


def _route_kernel(lg_ref, idx_ref, gslot_ref, *, cap):
    for s in range(lg_ref.shape[0]):
        _route_one(lg_ref.at[s], idx_ref.at[s], gslot_ref.at[s], cap)


def _route_one(lg_ref, idx_ref, gslot_ref, cap):
    lg = lg_ref[...]
    m = jnp.max(lg, axis=0, keepdims=True)
    ex = jnp.exp(lg - m)
    probs = ex / jnp.sum(ex, axis=0, keepdims=True)
    thr = jnp.zeros((lg.shape[0], 1), jnp.int32)
    for bit in range(30, -1, -1):
        cand = thr | (1 << bit)
        cnt = jnp.sum(jnp.where(probs >= pltpu.bitcast(cand, F32), 1.0, 0.0), axis=1, keepdims=True)
        thr = jnp.where(cnt >= cap, cand, thr)
    gt = probs >= pltpu.bitcast(thr + 1, F32)
    eq = jnp.logical_and(probs >= pltpu.bitcast(thr, F32), jnp.logical_not(gt))
    need = cap - jnp.sum(jnp.where(gt, 1.0, 0.0), axis=1, keepdims=True)
    eq_rank = _cumsum_lanes(jnp.where(eq, 1.0, 0.0))
    sel = jnp.where(gt, 1.0, jnp.where(eq, jnp.where(eq_rank <= need, 1.0, 0.0), 0.0))
    slot = (_cumsum_lanes(sel) - 1.0).astype(jnp.int32)
    lane = lax.broadcasted_iota(jnp.int32, slot.shape, 1)
    idx, gslot = _compact_lanes(slot, sel > 0.0, [lane, probs])
    idx_ref[...] = idx[:, :idx_ref.shape[-1]]
    gslot_ref[...] = gslot[:, :gslot_ref.shape[-1]]


ROUTE_SAMPLES = 4


def _route(logits, cap):
    b, e, n = logits.shape
    g = math.gcd(ROUTE_SAMPLES, b)
    capw = -(-cap // LANES) * LANES
    slots = pl.BlockSpec((g, e, capw), lambda i: (i, 0, 0))
    return pl.pallas_call(
        functools.partial(_route_kernel, cap=cap),
        grid=(b // g,),
        in_specs=[pl.BlockSpec((g, e, n), lambda i: (i, 0, 0))],
        out_specs=[slots, slots],
        out_shape=[jax.ShapeDtypeStruct((b, e, capw), jnp.int32),
                   jax.ShapeDtypeStruct((b, e, capw), F32)],
        compiler_params=_params(1),
        name="route",
    )(logits)


MOE_ROW_CHUNK = 512
SCATTER_GROUP = 4


def _moe_kernel(idx_ref, gs_ref, h_ref, x_ref, g2_ref, w1_ref, w3_ref, w2_ref, *rest, cap, final):
    n_extra = 1 if final else 0
    o_ref, xs_s, ys_s = rest[n_extra:n_extra + 3]
    e = pl.program_id(1)
    ne = pl.num_programs(1)
    n = h_ref.shape[0]
    tr = min(MOE_ROW_CHUNK, n)

    def gather_rows(k):
        for c in range(cap):
            xs_s[c:c + 1, :] = h_ref[pl.ds(idx_ref[k, c], 1), :]

    def scatter_rows(k, live):
        for c0 in range(0, cap, SCATTER_GROUP):
            rows = [pl.ds(idx_ref[k, c], 1) for c in range(c0, c0 + SCATTER_GROUP)]
            new = [o_ref[row, :] + (gs_ref[k, c0 + j] * live) * ys_s[c0 + j:c0 + j + 1, :]
                   for j, row in enumerate(rows)]
            for row, val in zip(rows, new):
                o_ref[row, :] = val

    @pl.when(e == 0)
    def _():
        o_ref[...] = x_ref[...]
        ys_s[...] = jnp.zeros_like(ys_s)
        gather_rows(0)

    words = xs_s[...]
    xs = jnp.concatenate(
        [pltpu.unpack_elementwise(words, index=i, packed_dtype=BF16, unpacked_dtype=F32) for i in (0, 1)],
        axis=1).astype(BF16)
    gather_rows(jnp.minimum(e + 1, ne - 1))
    scatter_rows(jnp.maximum(e - 1, 0), jnp.where(e > 0, 1.0, 0.0))
    a = jnp.dot(xs, w1_ref[...], preferred_element_type=F32)
    g = jnp.dot(xs, w3_ref[...], preferred_element_type=F32)
    y = jnp.dot((_silu(a) * g).astype(BF16), w2_ref[...], preferred_element_type=F32)
    ys_s[...] = y * g2_ref[...]

    @pl.when(e == ne - 1)
    def _():
        scatter_rows(ne - 1, 1.0)

    if final:
        @pl.when(e == ne - 1)
        def _():
            def norm_rows(ci, carry):
                rows = pl.ds(pl.multiple_of(ci * tr, tr), tr)
                xf = o_ref[rows, :]
                ms = jnp.mean(xf * xf, axis=-1, keepdims=True)
                o_ref[rows, :] = xf * lax.rsqrt(ms + EPS) * rest[0][...]
                return carry
            lax.fori_loop(0, n // tr, norm_rows, 0)


def _cast_kernel(*refs):
    k = len(refs) // 2
    for x_ref, o_ref in zip(refs[:k], refs[k:]):
        o_ref[...] = x_ref[...].astype(o_ref.dtype)


def _cast_bf16(*ws):
    g = ws[0].shape[0]
    specs = [pl.BlockSpec((None,) + w.shape[1:], lambda i: (i, 0, 0)) for w in ws]
    return pl.pallas_call(
        _cast_kernel,
        grid=(g,),
        in_specs=specs,
        out_specs=specs,
        out_shape=[jax.ShapeDtypeStruct(w.shape, BF16) for w in ws],
        compiler_params=_params(1),
        name="cast_bf16",
    )(*ws)


def _moe(h2, idx, gslot, x1, mods, w1, w3, w2, layer, final_g=None):
    b, n, d = x1.shape
    e = idx.shape[1]
    f = w1.shape[-1]
    cap = EC_CAPACITY * n // e
    final = final_g is not None
    table = pl.BlockSpec((None, e, idx.shape[-1]), lambda i, k: (i, 0, 0), memory_space=pltpu.SMEM)
    in_specs = [table, table,
                pl.BlockSpec((None, n, d // 2), lambda i, k: (i, 0, 0)),
                pl.BlockSpec((None, n, d), lambda i, k: (i, 0, 0)),
                pl.BlockSpec((None, 1, d), lambda i, k: (i * N_MOD + 5, 0, 0)),
                pl.BlockSpec((None, d, f), lambda i, k: (layer * e + k, 0, 0)),
                pl.BlockSpec((None, d, f), lambda i, k: (layer * e + k, 0, 0)),
                pl.BlockSpec((None, f, d), lambda i, k: (layer * e + k, 0, 0))]
    args = [idx, gslot, h2, x1, mods, w1, w3, w2]
    if final:
        in_specs.append(pl.BlockSpec((1, d), lambda i, k: (0, 0)))
        args.append(final_g.reshape(1, d))
    return pl.pallas_call(
        functools.partial(_moe_kernel, cap=cap, final=final),
        grid=(b, e),
        in_specs=in_specs,
        out_specs=pl.BlockSpec((None, n, d), lambda i, k: (i, 0, 0)),
        out_shape=jax.ShapeDtypeStruct((b, n, d), F32),
        scratch_shapes=[pltpu.VMEM((cap, d // 2), jnp.uint32), pltpu.VMEM((cap, d), F32)],
        compiler_params=_params(2),
        name="moe",
    )(*args)


VCONV_GROUP = 4


def _vconv_kernel(u_ref, w_ref, b_ref, lg_ref, lb_ref, o_ref, pad_s, y_s, *, n, k_taps, group):
    d = u_ref.shape[-1]
    groups = d // LANES
    n_rows = n // GRID_W
    top = (k_taps - 1) // 2
    halo = (group - 1) * GRID_W
    span = group * GRID_W
    tcp = min(256, n)

    for j in range(groups):
        if halo:
            pad_s[j, 0:halo, :] = jnp.zeros((halo, LANES), F32)
            pad_s[j, halo + n:halo + n + halo, :] = jnp.zeros((halo, LANES), F32)

    def fill(ci, carry):
        s = pl.multiple_of(ci * tcp, tcp)
        x = u_ref[pl.ds(s, tcp), :].astype(F32)
        for j in range(groups):
            pad_s[j, pl.ds(halo + s, tcp), :] = x[:, j * LANES:(j + 1) * LANES]
        return carry

    lax.fori_loop(0, n // tcp, fill, 0)

    for g in range(n_rows // group):
        r0 = g * group
        taps = [k for k in range(k_taps) if any(0 <= r0 + j + k - top < n_rows for j in range(group))]

        def lane_group(cj, carry, r0=r0, taps=taps):
            acc = None
            for k in taps:
                s = (r0 + k - top) * GRID_W + halo
                term = pad_s[cj, s:s + span, :] * w_ref[cj, k:k + 1, :]
                acc = term if acc is None else acc + term
            y_s[cj] = acc + b_ref[cj]
            return carry

        lax.fori_loop(0, groups, lane_group, 0)
        for j in range(group):
            acc = jnp.concatenate([y_s[cj, j * GRID_W:(j + 1) * GRID_W, :] for cj in range(groups)], axis=1)
            mu = jnp.mean(acc, axis=-1, keepdims=True)
            xc = acc - mu
            var = jnp.mean(xc * xc, axis=-1, keepdims=True)
            y = xc * lax.rsqrt(var + EPS) * lg_ref[...] + lb_ref[...]
            o_ref[(r0 + j) * GRID_W:(r0 + j + 1) * GRID_W, :] = _silu(y).astype(o_ref.dtype)


def _vconv_ln_silu(u, dw_w, dw_b, ln_g, ln_b):
    b, n, d = u.shape
    k_taps = dw_w.shape[0]
    groups = d // LANES
    group = math.gcd(VCONV_GROUP, n // GRID_W)
    halo = (group - 1) * GRID_W
    full = lambda shape: pl.BlockSpec(shape, lambda i: tuple(0 for _ in shape))
    w3 = dw_w.reshape(k_taps, groups, LANES).transpose(1, 0, 2)
    b3 = dw_b.reshape(groups, 1, LANES)
    return pl.pallas_call(
        functools.partial(_vconv_kernel, n=n, k_taps=k_taps, group=group),
        grid=(b,),
        in_specs=[pl.BlockSpec((None, n, d), lambda i: (i, 0, 0)),
                  full((groups, k_taps, LANES)), full((groups, 1, LANES)), full((1, d)), full((1, d))],
        out_specs=pl.BlockSpec((None, n, d), lambda i: (i, 0, 0)),
        out_shape=jax.ShapeDtypeStruct((b, n, d), BF16),
        scratch_shapes=[pltpu.VMEM((groups, n + 2 * halo, LANES), F32),
                        pltpu.VMEM((groups, group * GRID_W, LANES), F32)],
        compiler_params=_params(1),
        name="vconv_ln_silu",
    )(u, w3, b3, ln_g.reshape(1, d), ln_b.reshape(1, d))


def kernel(x, c, ctx, c_ctx, norm1_g, norm2_g, w_mod, b_mod, w_in, conv_a_w, conv_a_b, lru_wr, lru_br, lru_wi, lru_bi, lru_lam, conv_b_w, conv_b_b, filt_w1, filt_b1, filt_w2, filt_b2, filt_w3, filt_b3, filt_freq, filt_bias, w_out, cf_w1, cf_b1, cf_dw_w, cf_dw_b, cf_ln_g, cf_ln_b, cf_w2, cf_b2, router, exp_w1, exp_w3, exp_w2, final_g):
    bsz, n, d = x.shape
    depth = w_mod.shape[0]
    assert depth == 2 and w_in.shape[0] == 1 and cf_w1.shape[0] == 1
    rw = conv_a_w.shape[-1]
    hw = filt_bias.shape[-1]
    e = router.shape[-1]
    cap = EC_CAPACITY * n // e

    rows = -(-(bsz + 1) // SUBLANES) * SUBLANES
    cvec = jnp.concatenate([c, c_ctx[None], jnp.zeros((rows - bsz - 1, d), F32)], axis=0)
    mods_all = _adaln(cvec, w_mod, b_mod)
    mods = [mods_all[l].reshape(rows * N_MOD, 1, d) for l in range(depth)]

    w_in0 = w_in[0].astype(BF16)
    wg, bg = _lru_gate_weights(lru_wr[0], lru_wi[0], lru_br[0], lru_bi[0])
    ctx_xa = _norm_matmul(ctx, norm1_g[0], mods[0], lambda i: bsz, 0, 1, w_in0[:, rw:2 * rw],
                          out_dtype=BF16)
    h0_ctx = jnp.zeros((bsz, 2, rw), F32)
    _, ctx_state = _rglru(ctx_xa, 0, None, 0, conv_a_w[0], conv_a_b[0], wg, bg, lru_lam[0], h0_ctx)
    proj = _norm_matmul(x, norm1_g[0], mods[0], lambda i: i, 0, 1, w_in0, out_dtype=BF16)
    y_a, _ = _rglru(proj, 1, proj, 0, conv_a_w[0], conv_a_b[0], wg, bg, lru_lam[0], ctx_state,
                    out_dtype=BF16)

    fwd, bwd = _hyena_filters(n, filt_w1[0], filt_b1[0], filt_w2[0], filt_b2[0],
                              filt_w3[0], filt_b3[0], filt_freq[0])
    kpos, kneg = _polyphase_filters(fwd, bwd)
    cm, sm = (jnp.asarray(t) for t in _dft_tables(n // HYENA_PHASES))
    cm_hi, sm_hi = cm.astype(BF16), sm.astype(BF16)
    cm_lo = (cm - cm_hi.astype(F32)).astype(BF16)
    sm_lo = (sm - sm_hi.astype(F32)).astype(BF16)
    kr, ki, kn = _hyena_spectrum(cm_hi, cm_lo, sm_hi, sm_lo, kpos, kneg)
    z = _hyena(proj, 2 * rw, conv_b_w[0], conv_b_b[0], cm_hi, sm_hi,
               kr, ki, kn, filt_bias[0], out_dtype=BF16)

    ff = exp_w1.shape[-1]
    ew1, ew3, ew2 = _cast_bf16(exp_w1.reshape(depth * e, d, ff), exp_w3.reshape(depth * e, d, ff),
                               exp_w2.reshape(depth * e, ff, d))

    x1, h2, logits = _outproj([y_a, z], w_out[0].astype(BF16), None, x, mods[0], norm2_g[0], router[0])
    idx, gslot = _route(logits, cap)
    xl = _moe(h2, idx, gslot, x1, mods[0], ew1, ew3, ew2, 0)

    u = _norm_matmul(xl, norm1_g[1], mods[1], lambda i: i, 0, 1, cf_w1[0].astype(BF16),
                     bias=cf_b1[0], glu=True, out_dtype=BF16)
    v = _vconv_ln_silu(u, cf_dw_w[0], cf_dw_b[0], cf_ln_g[0], cf_ln_b[0])
    x1, h2, logits = _outproj([v], cf_w2[0].astype(BF16), cf_b2[0], xl, mods[1], norm2_g[1], router[1])
    idx, gslot = _route(logits, cap)
    return _moe(h2, idx, gslot, x1, mods[1], ew1, ew3, ew2, 1, final_g=final_g)
```

```python
import functools
import math

import numpy as np
import jax
import jax.numpy as jnp
from jax import lax
from jax.experimental import pallas as pl
from jax.experimental.pallas import tpu as pltpu

F32 = jnp.float32
BF16 = jnp.bfloat16
HIGHEST = lax.Precision.HIGHEST

EPS = 1e-6
GRID_W = 64
N_MOD = 6
LRU_HEADS = 8
LRU_C = 8.0
HYENA_ORDER = 2
HYENA_DECAY_TARGET = 1e-2
HYENA_FAST_PCT = 0.3
HYENA_SLOW_PCT = 1.5
HYENA_SHIFT = 0.05
EC_CAPACITY = 2

V7X_VMEM_BYTES = 64 * 1024 * 1024
VMEM_LIMIT = V7X_VMEM_BYTES - 4 * 1024 * 1024
SUBLANES = 8
LANES = 128


def _params(n_axes):
    return pltpu.CompilerParams(dimension_semantics=("arbitrary",) * n_axes,
                                vmem_limit_bytes=VMEM_LIMIT)


def _single(block_shape, index_map):
    return pl.BlockSpec(block_shape, index_map, pipeline_mode=pl.Buffered(1))


def _silu(x):
    return x * jax.nn.sigmoid(x)


def _split_hi_lo(x):
    hi = x.astype(BF16)
    lo = (x - hi.astype(F32)).astype(BF16)
    return jnp.concatenate([hi, lo], axis=-1)


def _norm_mod(xf, g, shift, scale):
    ms = jnp.mean(xf * xf, axis=-1, keepdims=True)
    y = xf * lax.rsqrt(ms + EPS) * g
    return y * (1.0 + scale) + shift


def _adaln_kernel(c_ref, w_ref, b_ref, o_ref):
    s = _silu(c_ref[...])
    o_ref[...] = jnp.dot(s, w_ref[...], precision=HIGHEST, preferred_element_type=F32) + b_ref[...]


def _adaln(cvec, w_mod, b_mod, tc=512):
    depth, d, dm = w_mod.shape
    r = cvec.shape[0]
    return pl.pallas_call(
        _adaln_kernel,
        grid=(depth, dm // tc),
        in_specs=[pl.BlockSpec((r, d), lambda l, j: (0, 0)),
                  pl.BlockSpec((None, d, tc), lambda l, j: (l, 0, j)),
                  pl.BlockSpec((None, 1, tc), lambda l, j: (l, 0, j))],
        out_specs=pl.BlockSpec((None, r, tc), lambda l, j: (l, 0, j)),
        out_shape=jax.ShapeDtypeStruct((depth, r, dm), F32),
        compiler_params=_params(2),
        name="adaln",
    )(cvec, w_mod, b_mod.reshape(depth, 1, dm))


def _norm_matmul_kernel(x_ref, g_ref, sh_ref, sc_ref, w_ref, *rest, has_bias, glu, n_cast):
    n_extra = 1 if has_bias else 0
    o_ref = rest[n_extra + n_cast]
    _ride_along_cast(rest[n_extra:n_extra + n_cast], rest[n_extra + n_cast + 1:])
    h = _norm_mod(x_ref[...], g_ref[...], sh_ref[...], sc_ref[...])
    y = jnp.dot(h.astype(BF16), w_ref[...], preferred_element_type=F32)
    if has_bias:
        y = y + rest[0][...]
    if glu:
        half = y.shape[-1] // 2
        y = y[:, :half] * jax.nn.sigmoid(y[:, half:])
    o_ref[...] = y.astype(o_ref.dtype)


def _norm_matmul(x, g, mods, mod_row, shift_j, scale_j, w_bf16, bias=None, glu=False, tn=1024,
                 out_dtype=F32, cast_ws=(), cast_part=(0, 1)):
    b, n, d = x.shape
    nout = w_bf16.shape[1]
    tn = min(tn, n)
    out_w = nout // 2 if glu else nout
    in_specs = [pl.BlockSpec((None, tn, d), lambda i, j: (i, j, 0)),
                pl.BlockSpec((1, d), lambda i, j: (0, 0)),
                pl.BlockSpec((None, 1, d), lambda i, j: (mod_row(i) * N_MOD + shift_j, 0, 0)),
                pl.BlockSpec((None, 1, d), lambda i, j: (mod_row(i) * N_MOD + scale_j, 0, 0)),
                _single((d, nout), lambda i, j: (0, 0))]
    args = [x, g.reshape(1, d), mods, mods, w_bf16]
    if bias is not None:
        in_specs.append(pl.BlockSpec((1, nout), lambda i, j: (0, 0)))
        args.append(bias.reshape(1, nout))
    nt = n // tn
    slab_in, slab_out, slab_shapes = _slab_specs(cast_ws, *cast_part, b * nt, lambda i, j: i * nt + j)
    outs = pl.pallas_call(
        functools.partial(_norm_matmul_kernel, has_bias=bias is not None, glu=glu, n_cast=len(cast_ws)),
        grid=(b, nt),
        in_specs=in_specs + slab_in,
        out_specs=[pl.BlockSpec((None, tn, out_w), lambda i, j: (i, j, 0))] + slab_out,
        out_shape=[jax.ShapeDtypeStruct((b, n, out_w), out_dtype)] + slab_shapes,
        compiler_params=_params(2),
        name="norm_matmul",
    )(*args, *cast_ws)
    return outs if cast_ws else outs[0]


HALO = 16


def _shift_rows(cur, k, prev, nxt, has_prev, has_next):
    t = cur.shape[0]
    row = lax.broadcasted_iota(jnp.int32, cur.shape, 0)
    if k < 0:
        out = pltpu.roll(cur, -k, 0)
        for r in range(-k):
            src = prev[HALO + k + r:HALO + k + r + 1, :]
            fill = jnp.where(has_prev, src, 0.0)
            out = jnp.where(row == r, fill, out)
    else:
        out = pltpu.roll(cur, t - k, 0)
        for r in range(k):
            src = nxt[r:r + 1, :]
            fill = jnp.where(has_next, src, 0.0)
            out = jnp.where(row == t - k + r, fill, out)
    return out


def _dwconv_chunk(x_ref, s, t, n, w_ref, b_ref, pad_left):
    cur = x_ref[pl.ds(s, t), :].astype(F32)
    prev = x_ref[pl.ds(pl.multiple_of(jnp.maximum(s - HALO, 0), HALO), HALO), :].astype(F32)
    nxt = x_ref[pl.ds(pl.multiple_of(jnp.minimum(s + t, n - HALO), HALO), HALO), :].astype(F32)
    has_prev = s > 0
    has_next = s + t < n
    k_taps = w_ref.shape[0]
    acc = None
    for k in range(k_taps):
        off = k - pad_left
        xs = cur if off == 0 else _shift_rows(cur, off, prev, nxt, has_prev, has_next)
        term = xs * w_ref[k:k + 1, :]
        acc = term if acc is None else acc + term
    return acc + b_ref[...]


GATE_CHUNK = 256
SCAN_CHUNK = 64


def _scan_rows(a, b, carry, reverse):
    t = a.shape[0]
    row = lax.broadcasted_iota(jnp.int32, a.shape, 0)
    s = 1
    while s < t:
        if reverse:
            valid = row < t - s
            sh = t - s
        else:
            valid = row >= s
            sh = s
        a_s = jnp.where(valid, pltpu.roll(a, sh, 0), 1.0)
        b_s = jnp.where(valid, pltpu.roll(b, sh, 0), 0.0)
        b = a * b_s + b
        a = a * a_s
        s *= 2
    h = b + a * carry
    return h, (h[0:1] if reverse else h[t - 1:t])


def _gelu_tanh(x):
    return 0.5 * x * (1.0 + jnp.tanh(math.sqrt(2.0 / math.pi) * (x + 0.044715 * (x * x * x))))


def _rglru_kernel(*refs, n, with_gate):
    if with_gate:
        (xa_ref, gate_ref, cw_ref, cb_ref, wg_ref, bg_ref, lam_ref, h0_ref,
         y_ref, hl_ref, af_s, uf_s, ab_s, ub_s) = refs
    else:
        (xa_ref, cw_ref, cb_ref, wg_ref, bg_ref, lam_ref, h0_ref,
         hl_ref, af_s, uf_s, ab_s, ub_s) = refs
    w = xa_ref.shape[-1]
    tg = min(GATE_CHUNK, n)
    ts = min(SCAN_CHUNK, n)
    lam = lam_ref[...]
    sp = jnp.maximum(-lam, 0.0) + jnp.log(1.0 + jnp.exp(-jnp.abs(lam)))
    half_c_sp = (-0.5 * LRU_C) * sp

    def gates(ci, carry):
        s = pl.multiple_of(ci * tg, tg)
        xa = _dwconv_chunk(xa_ref, s, tg, n, cw_ref, cb_ref, 1)
        z = jnp.dot(xa.astype(BF16), wg_ref[...], preferred_element_type=F32) + bg_ref[...]
        xh = 0.5 * xa
        for d, (a_s, u_s) in enumerate(((af_s, uf_s), (ab_s, ub_s))):
            tr = jnp.tanh(0.5 * z[:, (2 * d) * w:(2 * d + 1) * w])
            ti = jnp.tanh(0.5 * z[:, (2 * d + 1) * w:(2 * d + 2) * w])
            c1 = half_c_sp[d:d + 1, :]
            a = jnp.exp(c1 * tr + c1)
            t = 1.0 - a * a
            root = jnp.where(t > 0.0, t * lax.rsqrt(t), 0.0)
            a_s[pl.ds(s, tg), :] = a
            u_s[pl.ds(s, tg), :] = root * (ti * xh + xh)
        return carry

    lax.fori_loop(0, n // tg, gates, 0)

    nc = n // ts

    def fwd(ci, carry):
        s = pl.multiple_of(ci * ts, ts)
        h, carry = _scan_rows(af_s[pl.ds(s, ts), :], uf_s[pl.ds(s, ts), :], carry, False)
        uf_s[pl.ds(s, ts), :] = h
        return carry

    hf_last = lax.fori_loop(0, nc, fwd, h0_ref[0:1, :])

    def bwd(ci, carry):
        s = pl.multiple_of((nc - 1 - ci) * ts, ts)
        h, carry = _scan_rows(ab_s[pl.ds(s, ts), :], ub_s[pl.ds(s, ts), :], carry, True)
        if with_gate:
            hsum = uf_s[pl.ds(s, ts), :] + h
            gate = gate_ref[pl.ds(s, ts), :].astype(F32)
            y_ref[pl.ds(s, ts), :] = (_gelu_tanh(gate) * hsum).astype(y_ref.dtype)
        return carry

    hb_last = lax.fori_loop(0, nc, bwd, h0_ref[1:2, :])
    hl_ref[0:1, :] = hf_last
    hl_ref[1:2, :] = hb_last


def _rglru(xa_src, xa_col, gate_src, gate_col, conv_w, conv_b, wg_bf16, bg, lam, h0, out_dtype=F32):
    b, n, _ = xa_src.shape
    w = conv_w.shape[1]
    with_gate = gate_src is not None
    seq = lambda col: pl.BlockSpec((None, n, w), lambda i: (i, 0, col))
    full = lambda shape: pl.BlockSpec(shape, lambda i: tuple(0 for _ in shape))
    in_specs = [seq(xa_col)]
    args = [xa_src]
    if with_gate:
        in_specs.append(seq(gate_col))
        args.append(gate_src)
    in_specs += [full(conv_w.shape), full((1, w)), _single(wg_bf16.shape, lambda i: (0, 0)),
                 full((1, 4 * w)), full((2, w)), pl.BlockSpec((None, 2, w), lambda i: (i, 0, 0))]
    args += [conv_w, conv_b.reshape(1, w), wg_bf16, bg.reshape(1, 4 * w), lam, h0]
    out_specs = [pl.BlockSpec((None, 2, w), lambda i: (i, 0, 0))]
    out_shape = [jax.ShapeDtypeStruct((b, 2, w), F32)]
    if with_gate:
        out_specs.insert(0, pl.BlockSpec((None, n, w), lambda i: (i, 0, 0)))
        out_shape.insert(0, jax.ShapeDtypeStruct((b, n, w), out_dtype))
    outs = pl.pallas_call(
        functools.partial(_rglru_kernel, n=n, with_gate=with_gate),
        grid=(b,),
        in_specs=in_specs,
        out_specs=out_specs,
        out_shape=out_shape,
        scratch_shapes=[pltpu.VMEM((n, w), F32)] * 4,
        compiler_params=_params(1),
        name="rglru_gated" if with_gate else "rglru_state",
    )(*args)
    return (outs[0], outs[1]) if with_gate else (None, outs[0])


def _lru_gate_weights(lru_wr, lru_wi, lru_br, lru_bi):
    def dense(wh):
        heads, hd, _ = wh.shape
        eye = jnp.eye(heads, dtype=wh.dtype)
        return jnp.einsum('hij,hg->higj', wh, eye).reshape(heads * hd, heads * hd)
    wg = jnp.concatenate([dense(lru_wr[0]), dense(lru_wi[0]), dense(lru_wr[1]), dense(lru_wi[1])], axis=1)
    bg = jnp.concatenate([lru_br[0], lru_bi[0], lru_br[1], lru_bi[1]], axis=0)
    return wg.astype(BF16), bg


@functools.lru_cache(maxsize=None)
def _dft_tables(n):
    m = 2 * n
    k = (np.arange(n, dtype=np.int64)[:, None] * np.arange(n, dtype=np.int64)[None, :]) % m
    ang = 2.0 * np.pi * np.arange(m, dtype=np.float64) / m
    cm = np.cos(ang)[k].astype(np.float32)
    sm = np.sin(ang)[k].astype(np.float32)
    return cm, sm


@functools.lru_cache(maxsize=None)
def _filter_features(n, emb):
    t = np.linspace(0.0, 1.0, n, dtype=np.float32)[:, None]
    bands = (emb - 1) // 2
    w = (np.float32(2.0 * math.pi / n) * np.arange(n, dtype=np.float32))[:, None]
    f = np.linspace(1e-4, bands - 1, bands, dtype=np.float32)[None, :]
    z = np.concatenate([t, np.cos(f * w), -np.sin(f * w)], axis=-1).astype(np.float32)
    return z


@functools.lru_cache(maxsize=None)
def _hyena_deltas(width):
    d = np.abs(np.linspace(math.log(HYENA_DECAY_TARGET) / HYENA_SLOW_PCT,
                           math.log(HYENA_DECAY_TARGET) / HYENA_FAST_PCT, width, dtype=np.float32))
    return d.reshape(1, width).astype(np.float32)


def _filter_kernel(z_ref, t_ref, w1_ref, b1_ref, w2_ref, b2_ref, w3_ref, b3_ref, fr_ref, dl_ref,
                   fwd_ref, bwd_ref):
    hw = dl_ref.shape[-1]
    hid = jnp.sin(fr_ref[0:1, :] * (jnp.dot(z_ref[...], w1_ref[...], precision=HIGHEST,
                                            preferred_element_type=F32) + b1_ref[...]))
    hid = jnp.sin(fr_ref[1:2, :] * (jnp.dot(hid, w2_ref[...], precision=HIGHEST,
                                            preferred_element_type=F32) + b2_ref[...]))
    filt = jnp.dot(hid, w3_ref[...], precision=HIGHEST, preferred_element_type=F32) + b3_ref[...]
    window = jnp.exp(-t_ref[...] * dl_ref[...]) + HYENA_SHIFT
    for o in range(HYENA_ORDER):
        fwd_ref[o] = filt[:, (2 * o) * hw:(2 * o + 1) * hw] * window
        bwd_ref[o] = filt[:, (2 * o + 1) * hw:(2 * o + 2) * hw] * window


def _hyena_filters(n, filt_w1, filt_b1, filt_w2, filt_b2, filt_w3, filt_b3, filt_freq, tn=256):
    emb, hidden = filt_w1.shape
    hw = filt_w3.shape[1] // (2 * HYENA_ORDER)
    tn = min(tn, n)
    z = jnp.asarray(_filter_features(n, emb))
    t = jnp.asarray(np.linspace(0.0, 1.0, n, dtype=np.float32)[:, None])
    dl = jnp.asarray(_hyena_deltas(hw))
    full = lambda shape: pl.BlockSpec(shape, lambda j: tuple(0 for _ in shape))
    return pl.pallas_call(
        _filter_kernel,
        grid=(n // tn,),
        in_specs=[pl.BlockSpec((tn, emb), lambda j: (j, 0)),
                  pl.BlockSpec((tn, 1), lambda j: (j, 0)),
                  full((emb, hidden)), full((1, hidden)), full((hidden, hidden)), full((1, hidden)),
                  full((hidden, 2 * HYENA_ORDER * hw)), full((1, 2 * HYENA_ORDER * hw)),
                  full((2, hidden)), full((1, hw))],
        out_specs=[pl.BlockSpec((HYENA_ORDER, tn, hw), lambda j: (0, j, 0))] * 2,
        out_shape=[jax.ShapeDtypeStruct((HYENA_ORDER, n, hw), F32)] * 2,
        compiler_params=_params(1),
        name="hyena_filter",
    )(z, t, filt_w1, filt_b1.reshape(1, -1), filt_w2, filt_b2.reshape(1, -1),
      filt_w3, filt_b3.reshape(1, -1), filt_freq, dl)


def _dot_hi_lo(t_hi, t_lo, f_cat):
    hw = f_cat.shape[-1] // 2
    p = jnp.dot(t_hi, f_cat, preferred_element_type=F32)
    return p[:, :hw] + p[:, hw:] + jnp.dot(t_lo, f_cat[:, :hw], preferred_element_type=F32)


HYENA_PHASES = 4


def _polyphase_filters(fwd, bwd):
    r = HYENA_PHASES
    order, n, hw = fwd.shape
    fp = fwd.reshape(order, n // r, r, hw)
    bp = bwd.reshape(order, n // r, r, hw)

    def shift1(a, first):
        return jnp.concatenate([first[:, None], a[:, :-1]], axis=1)

    def no_lag0(a):
        return a.at[:, 0].set(0.0)

    zero = jnp.zeros((order, hw), fwd.dtype)
    pos, neg = [], []
    for d in range(-(r - 1), r):
        if d == 0:
            pos.append(fp[:, :, 0])
            neg.append(no_lag0(bp[:, :, 0]))
        elif d > 0:
            pos.append(fp[:, :, d])
            neg.append(shift1(bp[:, :, r - d], zero))
        else:
            pos.append(shift1(fp[:, :, r + d], bp[:, 0, -d]))
            neg.append(no_lag0(bp[:, :, -d]))
    return jnp.stack(pos, axis=1), jnp.stack(neg, axis=1)


def _spectrum_kernel(ch_ref, cl_ref, sh_ref, sl_ref, kp_ref, km_ref, kr_ref, ki_ref, kn_ref):
    m = kp_ref.shape[0]
    ksum = kp_ref[...] + km_ref[...]
    kdif = km_ref[...] - kp_ref[...]
    row = lax.broadcasted_iota(jnp.int32, ksum.shape, 0)
    scale = jnp.where(row == 0, 1.0 / (2 * m), 2.0 / (2 * m))
    kr_ref[...] = scale * _dot_hi_lo(ch_ref[...], cl_ref[...], _split_hi_lo(ksum))
    ki_ref[...] = scale * _dot_hi_lo(sh_ref[...], sl_ref[...], _split_hi_lo(kdif))
    alt = jnp.where((row & 1) == 0, 1.0, -1.0)
    kn_ref[...] = jnp.sum(ksum * alt, axis=0, keepdims=True) * (1.0 / (2 * m))


def _hyena_spectrum(c_hi, c_lo, s_hi, s_lo, kpos, kneg):
    order, nd, m, hw = kpos.shape
    tab = pl.BlockSpec((m, m), lambda o, d: (0, 0))
    filt = pl.BlockSpec((None, None, m, hw), lambda o, d: (o, d, 0, 0))
    row = pl.BlockSpec((None, None, 1, hw), lambda o, d: (o, d, 0, 0))
    return pl.pallas_call(
        _spectrum_kernel,
        grid=(order, nd),
        in_specs=[tab, tab, tab, tab, filt, filt],
        out_specs=[filt, filt, row],
        out_shape=[jax.ShapeDtypeStruct((order, nd, m, hw), F32)] * 2
        + [jax.ShapeDtypeStruct((order, nd, 1, hw), F32)],
        compiler_params=_params(2),
        name="hyena_spectrum",
    )(c_hi, c_lo, s_hi, s_lo, kpos, kneg)


HYENA_CHUNK = 512


def _ride_along_cast(src_refs, dst_refs):
    for src, dst in zip(src_refs, dst_refs):
        dst[...] = src[...].astype(dst.dtype)


def _slab_specs(ws, part, parts, steps, step_of):
    ins, outs, shapes = [], [], []
    for w in ws:
        rows, cols = w.shape
        slab = rows // (parts * steps)
        ins.append(pl.BlockSpec((slab, cols), lambda *ids: (part * steps + step_of(*ids), 0)))
        outs.append(pl.BlockSpec((slab, cols), lambda *ids: (step_of(*ids), 0)))
        shapes.append(jax.ShapeDtypeStruct((rows // parts, cols), BF16))
    return ins, outs, shapes


def _hyena_kernel(v_ref, g0_ref, g1_ref, cwv_ref, cw0_ref, cw1_ref, cbv_ref, cb0_ref, cb1_ref,
                  cm_ref, sm_ref, kr_ref, ki_ref, kn_ref, fb_ref, *rest, n, n_cast):
    cast_src, z_ref, cast_dst = rest[:n_cast], rest[n_cast], rest[n_cast + 1:2 * n_cast + 1]
    u_s, g_s, x_s = rest[2 * n_cast + 1:]
    _ride_along_cast(cast_src, cast_dst)
    r = HYENA_PHASES
    ct = v_ref.shape[-1]
    m = n // r
    tc = min(HYENA_CHUNK, n)
    row = lax.broadcasted_iota(jnp.int32, (m, ct), 0)
    alt = jnp.where((row & 1) == 0, 1.0, -1.0)
    rc = min(128, m)

    groups = ct // LANES

    def put_rows(dst_s, rows, val):
        for j in range(groups):
            dst_s[j, rows, :] = val[:, j * LANES:(j + 1) * LANES]

    def get_rows(src_s, rows):
        return jnp.concatenate([src_s[j, rows, :] for j in range(groups)], axis=1)

    def short_conv(src_ref, cw_ref, cb_ref, dst_s):
        def body(ci, carry):
            s = pl.multiple_of(ci * tc, tc)
            put_rows(dst_s, pl.ds(s, tc), _dwconv_chunk(src_ref, s, tc, n, cw_ref, cb_ref, 1))
            return carry
        lax.fori_loop(0, n // tc, body, 0)

    def phase(ref, p):
        return get_rows(ref, pl.ds(p, m, stride=r))

    short_conv(v_ref, cwv_ref, cbv_ref, u_s)
    for o, (g_ref, cw_ref, cb_ref) in enumerate(((g0_ref, cw0_ref, cb0_ref), (g1_ref, cw1_ref, cb1_ref))):
        short_conv(g_ref, cw_ref, cb_ref, g_s)
        nyq = []
        for q in range(r):
            uq = phase(u_s, q)
            ub = uq.astype(BF16)
            x_s[q, 0] = jnp.dot(cm_ref[...], ub, preferred_element_type=F32)
            x_s[q, 1] = jnp.dot(sm_ref[...], ub, preferred_element_type=F32)
            nyq.append(jnp.sum(uq * alt, axis=0, keepdims=True))
        for p in range(r):
            zc_parts, zs_parts = [], []
            for r0 in range(0, m, rc):
                zc = zs = None
                for q in range(r):
                    d = p - q + r - 1
                    xre = x_s[q, 0, r0:r0 + rc, :]
                    xim = x_s[q, 1, r0:r0 + rc, :]
                    kr = kr_ref[o, d, r0:r0 + rc, :]
                    ki = ki_ref[o, d, r0:r0 + rc, :]
                    tc_ = xre * kr + xim * ki
                    ts_ = xim * kr - xre * ki
                    zc = tc_ if zc is None else zc + tc_
                    zs = ts_ if zs is None else zs + ts_
                zc_parts.append(zc.astype(BF16))
                zs_parts.append(zs.astype(BF16))
            zc = jnp.concatenate(zc_parts, axis=0)
            zs = jnp.concatenate(zs_parts, axis=0)
            y = (jnp.dot(cm_ref[...], zc, preferred_element_type=F32)
                 + jnp.dot(sm_ref[...], zs, preferred_element_type=F32))
            ynyq = None
            for q in range(r):
                t = nyq[q] * kn_ref[o, p - q + r - 1]
                ynyq = t if ynyq is None else ynyq + t
            y = y + alt * ynyq + phase(u_s, p) * fb_ref[o]
            put_rows(u_s, pl.ds(p, m, stride=r), phase(g_s, p) * y)

    def emit(ci, carry):
        rows = pl.ds(pl.multiple_of(ci * tc, tc), tc)
        z_ref[rows, :] = get_rows(u_s, rows).astype(z_ref.dtype)
        return carry

    lax.fori_loop(0, n // tc, emit, 0)


def _hyena(proj, hy_col0, conv_w, conv_b, cm_bf16, sm_bf16, kr, ki, kn, filt_bias, ct=256,
           out_dtype=F32, cast_ws=(), cast_part=(0, 1)):
    b, n, _ = proj.shape
    hw = kr.shape[-1]
    nd, m = kr.shape[1], kr.shape[2]
    ct = min(ct, hw)
    nct = hw // ct
    base = hy_col0 // ct
    k_taps = conv_w.shape[0]

    def seq(j):
        return pl.BlockSpec((None, n, ct), lambda c, i: (i, 0, base + j * nct + c))

    def cw(j):
        return pl.BlockSpec((k_taps, ct), lambda c, i: (0, j * nct + c))

    def cb(j):
        return pl.BlockSpec((1, ct), lambda c, i: (0, j * nct + c))

    kspec = _single((HYENA_ORDER, nd, m, ct), lambda c, i: (0, 0, 0, c))
    knspec = pl.BlockSpec((HYENA_ORDER, nd, 1, ct), lambda c, i: (0, 0, 0, c))
    rowspec = pl.BlockSpec((HYENA_ORDER, 1, ct), lambda c, i: (0, 0, c))
    conv_b2 = conv_b.reshape(1, -1)
    slab_in, slab_out, slab_shapes = _slab_specs(cast_ws, *cast_part, nct * b, lambda c, i: c * b + i)
    return pl.pallas_call(
        functools.partial(_hyena_kernel, n=n, n_cast=len(cast_ws)),
        grid=(nct, b),
        in_specs=[seq(0), seq(1), seq(2), cw(0), cw(1), cw(2), cb(0), cb(1), cb(2),
                  pl.BlockSpec((m, m), lambda c, i: (0, 0)), pl.BlockSpec((m, m), lambda c, i: (0, 0)),
                  kspec, kspec, knspec, rowspec] + slab_in,
        out_specs=[pl.BlockSpec((None, n, ct), lambda c, i: (i, 0, c))] + slab_out,
        out_shape=[jax.ShapeDtypeStruct((b, n, hw), out_dtype)] + slab_shapes,
        scratch_shapes=[pltpu.VMEM((ct // LANES, n, LANES), F32), pltpu.VMEM((ct // LANES, n, LANES), F32),
                        pltpu.VMEM((HYENA_PHASES, 2, m, ct), F32)],
        compiler_params=_params(2),
        name="hyena",
    )(proj, proj, proj, conv_w, conv_w, conv_w, conv_b2, conv_b2, conv_b2,
      cm_bf16, sm_bf16, kr, ki, kn, filt_bias.reshape(HYENA_ORDER, 1, hw), *cast_ws)


def _outproj_kernel(*refs, n_act, has_bias):
    acts = refs[:n_act]
    (w_ref, x_ref, g1_ref, ng_ref, sh_ref, sc_ref, rt_ref) = refs[n_act:n_act + 7]
    rest = refs[n_act + 7:]
    x1_ref, h2_ref, lg_ref = rest[-3:]
    y = None
    k0 = 0
    for a_ref in acts:
        k = a_ref.shape[-1]
        part = jnp.dot(a_ref[...].astype(BF16), w_ref[k0:k0 + k, :], preferred_element_type=F32)
        y = part if y is None else y + part
        k0 += k
    if has_bias:
        y = y + rest[0][...]
    x1 = x_ref[...] + g1_ref[...] * y
    x1_ref[...] = x1
    h2 = _norm_mod(x1, ng_ref[...], sh_ref[...], sc_ref[...])
    h_hi = h2.astype(BF16)
    h_lo = (h2 - h_hi.astype(F32)).astype(BF16)
    half = h2.shape[-1] // 2
    h2_ref[...] = pltpu.pack_elementwise([h2[:, :half], h2[:, half:]], packed_dtype=BF16)
    ne = lg_ref.shape[0]
    nt = (((1,), (1,)), ((), ()))
    p_hi = lax.dot_general(rt_ref[...], h_hi, nt, preferred_element_type=F32)
    p_lo = lax.dot_general(rt_ref[0:ne, :], h_lo, nt, preferred_element_type=F32)
    lg_ref[...] = p_hi[0:ne, :] + p_hi[ne:2 * ne, :] + p_lo


def _outproj(acts, w_bf16, bias, x, mods, norm_g, router, tn=512):
    b, n, d = x.shape
    e = router.shape[1]
    tn = min(tn, n)
    row = lambda j: pl.BlockSpec((None, 1, d), lambda i, t: (i * N_MOD + j, 0, 0))
    in_specs = [pl.BlockSpec((None, tn, a.shape[-1]), lambda i, t: (i, t, 0)) for a in acts]
    in_specs += [_single(w_bf16.shape, lambda i, t: (0, 0)),
                 pl.BlockSpec((None, tn, d), lambda i, t: (i, t, 0)),
                 row(2), pl.BlockSpec((1, d), lambda i, t: (0, 0)), row(3), row(4),
                 pl.BlockSpec((2 * e, d), lambda i, t: (0, 0))]
    rt = router.T
    rt_hi = rt.astype(BF16)
    rt_lo = (rt - rt_hi.astype(F32)).astype(BF16)
    args = list(acts) + [w_bf16, x, mods, norm_g.reshape(1, d), mods, mods,
                         jnp.concatenate([rt_hi, rt_lo], axis=0)]
    if bias is not None:
        in_specs.append(pl.BlockSpec((1, d), lambda i, t: (0, 0)))
        args.append(bias.reshape(1, d))
    return pl.pallas_call(
        functools.partial(_outproj_kernel, n_act=len(acts), has_bias=bias is not None),
        grid=(b, n // tn),
        in_specs=in_specs,
        out_specs=[pl.BlockSpec((None, tn, d), lambda i, t: (i, t, 0)),
                   pl.BlockSpec((None, tn, d // 2), lambda i, t: (i, t, 0)),
                   pl.BlockSpec((None, e, tn), lambda i, t: (i, 0, t))],
        out_shape=[jax.ShapeDtypeStruct((b, n, d), F32),
                   jax.ShapeDtypeStruct((b, n, d // 2), jnp.uint32),
                   jax.ShapeDtypeStruct((b, e, n), F32)],
        compiler_params=_params(2),
        name="outproj_router",
    )(*args)


def _cumsum_lanes(x):
    n = x.shape[-1]
    lane = lax.broadcasted_iota(jnp.int32, x.shape, 1)
    s = 1
    while s < n:
        x = x + jnp.where(lane >= s, pltpu.roll(x, s, 1), 0)
        s *= 2
    return x


def _compact_lanes(slot, sel, payloads):
    n = slot.shape[-1]
    lane = lax.broadcasted_iota(jnp.int32, slot.shape, 1)
    enc = jnp.where(sel, ((lane - slot) << 1) | 1, 0)
    j = 0
    while (1 << j) < n:
        sh = 1 << j
        inside = lane < n - sh
        src_enc = jnp.where(inside, pltpu.roll(enc, n - sh, 1), 0)
        take = ((src_enc >> (j + 1)) & 1) > 0
        stay = (enc & (1 - ((enc >> (j + 1)) & 1))) > 0
        payloads = [jnp.where(take, pltpu.roll(v, n - sh, 1), v) for v in payloads]
        enc = jnp.where(take, src_enc, jnp.where(stay, enc, 0))
        j += 1
    return payloads


def _route_kernel(lg_ref, idx_ref, gslot_ref, *, cap):
    for s in range(lg_ref.shape[0]):
        _route_one(lg_ref.at[s], idx_ref.at[s], gslot_ref.at[s], cap)


def _route_one(lg_ref, idx_ref, gslot_ref, cap):
    lg = lg_ref[...]
    m = jnp.max(lg, axis=0, keepdims=True)
    ex = jnp.exp(lg - m)
    probs = ex / jnp.sum(ex, axis=0, keepdims=True)
    thr = jnp.zeros((lg.shape[0], 1), jnp.int32)
    for bit in range(30, -1, -1):
        cand = thr | (1 << bit)
        cnt = jnp.sum(jnp.where(probs >= pltpu.bitcast(cand, F32), 1.0, 0.0), axis=1, keepdims=True)
        thr = jnp.where(cnt >= cap, cand, thr)
    gt = probs >= pltpu.bitcast(thr + 1, F32)
    eq = jnp.logical_and(probs >= pltpu.bitcast(thr, F32), jnp.logical_not(gt))
    need = cap - jnp.sum(jnp.where(gt, 1.0, 0.0), axis=1, keepdims=True)
    eq_rank = _cumsum_lanes(jnp.where(eq, 1.0, 0.0))
    sel = jnp.where(gt, 1.0, jnp.where(eq, jnp.where(eq_rank <= need, 1.0, 0.0), 0.0))
    slot = (_cumsum_lanes(sel) - 1.0).astype(jnp.int32)
    lane = lax.broadcasted_iota(jnp.int32, slot.shape, 1)
    idx, gslot = _compact_lanes(slot, sel > 0.0, [lane, probs])
    idx_ref[...] = idx[:, :idx_ref.shape[-1]]
    gslot_ref[...] = gslot[:, :gslot_ref.shape[-1]]


ROUTE_SAMPLES = 4


def _route(logits, cap):
    b, e, n = logits.shape
    g = math.gcd(ROUTE_SAMPLES, b)
    capw = -(-cap // LANES) * LANES
    slots = pl.BlockSpec((g, e, capw), lambda i: (i, 0, 0))
    return pl.pallas_call(
        functools.partial(_route_kernel, cap=cap),
        grid=(b // g,),
        in_specs=[pl.BlockSpec((g, e, n), lambda i: (i, 0, 0))],
        out_specs=[slots, slots],
        out_shape=[jax.ShapeDtypeStruct((b, e, capw), jnp.int32),
                   jax.ShapeDtypeStruct((b, e, capw), F32)],
        compiler_params=_params(1),
        name="route",
    )(logits)


MOE_ROW_CHUNK = 512
SCATTER_GROUP = 4


def _moe_kernel(idx_ref, gs_ref, h_ref, x_ref, g2_ref, w1_ref, w3_ref, w2_ref, *rest, cap, final,
                n_cast):
    n_extra = 1 if final else 0
    cast_src = rest[n_extra:n_extra + n_cast]
    o_ref = rest[n_extra + n_cast]
    cast_dst = rest[n_extra + n_cast + 1:n_extra + 2 * n_cast + 1]
    xs_s, ys_s = rest[n_extra + 2 * n_cast + 1:]
    _ride_along_cast(cast_src, cast_dst)
    e = pl.program_id(1)
    ne = pl.num_programs(1)
    n = h_ref.shape[0]
    tr = min(MOE_ROW_CHUNK, n)

    def gather_rows(k):
        for c in range(cap):
            xs_s[c:c + 1, :] = h_ref[pl.ds(idx_ref[k, c], 1), :]

    def scatter_rows(k, live):
        for c0 in range(0, cap, SCATTER_GROUP):
            rows = [pl.ds(idx_ref[k, c], 1) for c in range(c0, c0 + SCATTER_GROUP)]
            new = [o_ref[row, :] + (gs_ref[k, c0 + j] * live) * ys_s[c0 + j:c0 + j + 1, :]
                   for j, row in enumerate(rows)]
            for row, val in zip(rows, new):
                o_ref[row, :] = val

    @pl.when(e == 0)
    def _():
        o_ref[...] = x_ref[...]
        ys_s[...] = jnp.zeros_like(ys_s)
        gather_rows(0)

    words = xs_s[...]
    xs = jnp.concatenate(
        [pltpu.unpack_elementwise(words, index=i, packed_dtype=BF16, unpacked_dtype=F32) for i in (0, 1)],
        axis=1).astype(BF16)
    gather_rows(jnp.minimum(e + 1, ne - 1))
    scatter_rows(jnp.maximum(e - 1, 0), jnp.where(e > 0, 1.0, 0.0))
    a = jnp.dot(xs, w1_ref[...], preferred_element_type=F32)
    g = jnp.dot(xs, w3_ref[...], preferred_element_type=F32)
    y = jnp.dot((_silu(a) * g).astype(BF16), w2_ref[...], preferred_element_type=F32)
    ys_s[...] = y * g2_ref[...]

    @pl.when(e == ne - 1)
    def _():
        scatter_rows(ne - 1, 1.0)

    if final:
        @pl.when(e == ne - 1)
        def _():
            def norm_rows(ci, carry):
                rows = pl.ds(pl.multiple_of(ci * tr, tr), tr)
                xf = o_ref[rows, :]
                ms = jnp.mean(xf * xf, axis=-1, keepdims=True)
                o_ref[rows, :] = xf * lax.rsqrt(ms + EPS) * rest[0][...]
                return carry
            lax.fori_loop(0, n // tr, norm_rows, 0)


def _moe(h2, idx, gslot, x1, mods, w1, w3, w2, final_g=None, cast_ws=(), cast_part=(0, 1)):
    b, n, d = x1.shape
    e = idx.shape[1]
    f = w1.shape[-1]
    cap = EC_CAPACITY * n // e
    final = final_g is not None
    table = pl.BlockSpec((None, e, idx.shape[-1]), lambda i, k: (i, 0, 0), memory_space=pltpu.SMEM)
    in_specs = [table, table,
                pl.BlockSpec((None, n, d // 2), lambda i, k: (i, 0, 0)),
                pl.BlockSpec((None, n, d), lambda i, k: (i, 0, 0)),
                pl.BlockSpec((None, 1, d), lambda i, k: (i * N_MOD + 5, 0, 0)),
                pl.BlockSpec((None, d, f), lambda i, k: (k, 0, 0)),
                pl.BlockSpec((None, d, f), lambda i, k: (k, 0, 0)),
                pl.BlockSpec((None, f, d), lambda i, k: (k, 0, 0))]
    args = [idx, gslot, h2, x1, mods, w1, w3, w2]
    if final:
        in_specs.append(pl.BlockSpec((1, d), lambda i, k: (0, 0)))
        args.append(final_g.reshape(1, d))
    slab_in, slab_out, slab_shapes = _slab_specs(cast_ws, *cast_part, b * e, lambda i, k: i * e + k)
    outs = pl.pallas_call(
        functools.partial(_moe_kernel, cap=cap, final=final, n_cast=len(cast_ws)),
        grid=(b, e),
        in_specs=in_specs + slab_in,
        out_specs=[pl.BlockSpec((None, n, d), lambda i, k: (i, 0, 0))] + slab_out,
        out_shape=[jax.ShapeDtypeStruct((b, n, d), F32)] + slab_shapes,
        scratch_shapes=[pltpu.VMEM((cap, d // 2), jnp.uint32), pltpu.VMEM((cap, d), F32)],
        compiler_params=_params(2),
        name="moe",
    )(*args, *cast_ws)
    return outs if cast_ws else outs[0]


VCONV_GROUP = 4


def _vconv_kernel(u_ref, w_ref, b_ref, lg_ref, lb_ref, o_ref, pad_s, y_s, *, n, k_taps, group):
    d = u_ref.shape[-1]
    groups = d // LANES
    n_rows = n // GRID_W
    top = (k_taps - 1) // 2
    halo = (group - 1) * GRID_W
    span = group * GRID_W
    tcp = min(256, n)

    for j in range(groups):
        if halo:
            pad_s[j, 0:halo, :] = jnp.zeros((halo, LANES), F32)
            pad_s[j, halo + n:halo + n + halo, :] = jnp.zeros((halo, LANES), F32)

    def fill(ci, carry):
        s = pl.multiple_of(ci * tcp, tcp)
        x = u_ref[pl.ds(s, tcp), :].astype(F32)
        for j in range(groups):
            pad_s[j, pl.ds(halo + s, tcp), :] = x[:, j * LANES:(j + 1) * LANES]
        return carry

    lax.fori_loop(0, n // tcp, fill, 0)

    for g in range(n_rows // group):
        r0 = g * group
        taps = [k for k in range(k_taps) if any(0 <= r0 + j + k - top < n_rows for j in range(group))]

        def lane_group(cj, carry, r0=r0, taps=taps):
            acc = None
            for k in taps:
                s = (r0 + k - top) * GRID_W + halo
                term = pad_s[cj, s:s + span, :] * w_ref[cj, k:k + 1, :]
                acc = term if acc is None else acc + term
            y_s[cj] = acc + b_ref[cj]
            return carry

        lax.fori_loop(0, groups, lane_group, 0)
        for j in range(group):
            acc = jnp.concatenate([y_s[cj, j * GRID_W:(j + 1) * GRID_W, :] for cj in range(groups)], axis=1)
            mu = jnp.mean(acc, axis=-1, keepdims=True)
            xc = acc - mu
            var = jnp.mean(xc * xc, axis=-1, keepdims=True)
            y = xc * lax.rsqrt(var + EPS) * lg_ref[...] + lb_ref[...]
            o_ref[(r0 + j) * GRID_W:(r0 + j + 1) * GRID_W, :] = _silu(y).astype(o_ref.dtype)


def _vconv_ln_silu(u, dw_w, dw_b, ln_g, ln_b):
    b, n, d = u.shape
    k_taps = dw_w.shape[0]
    groups = d // LANES
    group = math.gcd(VCONV_GROUP, n // GRID_W)
    halo = (group - 1) * GRID_W
    full = lambda shape: pl.BlockSpec(shape, lambda i: tuple(0 for _ in shape))
    w3 = dw_w.reshape(k_taps, groups, LANES).transpose(1, 0, 2)
    b3 = dw_b.reshape(groups, 1, LANES)
    return pl.pallas_call(
        functools.partial(_vconv_kernel, n=n, k_taps=k_taps, group=group),
        grid=(b,),
        in_specs=[pl.BlockSpec((None, n, d), lambda i: (i, 0, 0)),
                  full((groups, k_taps, LANES)), full((groups, 1, LANES)), full((1, d)), full((1, d))],
        out_specs=pl.BlockSpec((None, n, d), lambda i: (i, 0, 0)),
        out_shape=jax.ShapeDtypeStruct((b, n, d), BF16),
        scratch_shapes=[pltpu.VMEM((groups, n + 2 * halo, LANES), F32),
                        pltpu.VMEM((groups, group * GRID_W, LANES), F32)],
        compiler_params=_params(1),
        name="vconv_ln_silu",
    )(u, w3, b3, ln_g.reshape(1, d), ln_b.reshape(1, d))


def kernel(x, c, ctx, c_ctx, norm1_g, norm2_g, w_mod, b_mod, w_in, conv_a_w, conv_a_b, lru_wr, lru_br, lru_wi, lru_bi, lru_lam, conv_b_w, conv_b_b, filt_w1, filt_b1, filt_w2, filt_b2, filt_w3, filt_b3, filt_freq, filt_bias, w_out, cf_w1, cf_b1, cf_dw_w, cf_dw_b, cf_ln_g, cf_ln_b, cf_w2, cf_b2, router, exp_w1, exp_w3, exp_w2, final_g):
    bsz, n, d = x.shape
    depth = w_mod.shape[0]
    assert depth == 2 and w_in.shape[0] == 1 and cf_w1.shape[0] == 1
    rw = conv_a_w.shape[-1]
    hw = filt_bias.shape[-1]
    e = router.shape[-1]
    cap = EC_CAPACITY * n // e

    rows = -(-(bsz + 1) // SUBLANES) * SUBLANES
    cvec = jnp.concatenate([c, c_ctx[None], jnp.zeros((rows - bsz - 1, d), F32)], axis=0)
    mods_all = _adaln(cvec, w_mod, b_mod)
    mods = [mods_all[l].reshape(rows * N_MOD, 1, d) for l in range(depth)]

    w_in0 = w_in[0].astype(BF16)
    wg, bg = _lru_gate_weights(lru_wr[0], lru_wi[0], lru_br[0], lru_bi[0])
    ctx_xa = _norm_matmul(ctx, norm1_g[0], mods[0], lambda i: bsz, 0, 1, w_in0[:, rw:2 * rw],
                          out_dtype=BF16)
    h0_ctx = jnp.zeros((bsz, 2, rw), F32)
    _, ctx_state = _rglru(ctx_xa, 0, None, 0, conv_a_w[0], conv_a_b[0], wg, bg, lru_lam[0], h0_ctx)
    proj = _norm_matmul(x, norm1_g[0], mods[0], lambda i: i, 0, 1, w_in0, out_dtype=BF16)
    y_a, _ = _rglru(proj, 1, proj, 0, conv_a_w[0], conv_a_b[0], wg, bg, lru_lam[0], ctx_state,
                    out_dtype=BF16)

    fwd, bwd = _hyena_filters(n, filt_w1[0], filt_b1[0], filt_w2[0], filt_b2[0],
                              filt_w3[0], filt_b3[0], filt_freq[0])
    kpos, kneg = _polyphase_filters(fwd, bwd)
    cm, sm = (jnp.asarray(t) for t in _dft_tables(n // HYENA_PHASES))
    cm_hi, sm_hi = cm.astype(BF16), sm.astype(BF16)
    cm_lo = (cm - cm_hi.astype(F32)).astype(BF16)
    sm_lo = (sm - sm_hi.astype(F32)).astype(BF16)
    kr, ki, kn = _hyena_spectrum(cm_hi, cm_lo, sm_hi, sm_lo, kpos, kneg)
    ff = exp_w1.shape[-1]
    expert_ws = [exp_w1.reshape(-1, ff), exp_w3.reshape(-1, ff), exp_w2.reshape(-1, d)]
    expert_shapes = [(e, d, ff), (e, d, ff), (e, ff, d)]
    z, *ew = _hyena(proj, 2 * rw, conv_b_w[0], conv_b_b[0], cm_hi, sm_hi,
                    kr, ki, kn, filt_bias[0], out_dtype=BF16, cast_ws=expert_ws, cast_part=(0, depth))
    ew = [w.reshape(s) for w, s in zip(ew, expert_shapes)]

    x1, h2, logits = _outproj([y_a, z], w_out[0].astype(BF16), None, x, mods[0], norm2_g[0], router[0])
    idx, gslot = _route(logits, cap)
    xl = _moe(h2, idx, gslot, x1, mods[0], *ew)

    u, *ew_next = _norm_matmul(xl, norm1_g[1], mods[1], lambda i: i, 0, 1, cf_w1[0].astype(BF16),
                               bias=cf_b1[0], glu=True, out_dtype=BF16,
                               cast_ws=expert_ws, cast_part=(1, depth))
    ew_next = [w.reshape(s) for w, s in zip(ew_next, expert_shapes)]
    v = _vconv_ln_silu(u, cf_dw_w[0], cf_dw_b[0], cf_ln_g[0], cf_ln_b[0])
    x1, h2, logits = _outproj([v], cf_w2[0].astype(BF16), cf_b2[0], xl, mods[1], norm2_g[1], router[1])
    idx, gslot = _route(logits, cap)
    return _moe(h2, idx, gslot, x1, mods[1], *ew_next, final_g=final_g)
```

```python
import functools
import math

import numpy as np
import jax
import jax.numpy as jnp
from jax import lax
from jax.experimental import pallas as pl
from jax.experimental.pallas import tpu as pltpu

F32 = jnp.float32
BF16 = jnp.bfloat16
HIGHEST = lax.Precision.HIGHEST

EPS = 1e-6
GRID_W = 64
N_MOD = 6
LRU_HEADS = 8
LRU_C = 8.0
HYENA_ORDER = 2
HYENA_DECAY_TARGET = 1e-2
HYENA_FAST_PCT = 0.3
HYENA_SLOW_PCT = 1.5
HYENA_SHIFT = 0.05
EC_CAPACITY = 2

V7X_VMEM_BYTES = 64 * 1024 * 1024
VMEM_LIMIT = V7X_VMEM_BYTES - 4 * 1024 * 1024
SUBLANES = 8
LANES = 128


def _params(n_axes):
    return pltpu.CompilerParams(dimension_semantics=("arbitrary",) * n_axes,
                                vmem_limit_bytes=VMEM_LIMIT)


def _single(block_shape, index_map):
    return pl.BlockSpec(block_shape, index_map, pipeline_mode=pl.Buffered(1))


def _silu(x):
    return x * jax.nn.sigmoid(x)


def _split_hi_lo(x):
    hi = x.astype(BF16)
    lo = (x - hi.astype(F32)).astype(BF16)
    return jnp.concatenate([hi, lo], axis=-1)


def _norm_mod(xf, g, shift, scale):
    ms = jnp.mean(xf * xf, axis=-1, keepdims=True)
    y = xf * lax.rsqrt(ms + EPS) * g
    return y * (1.0 + scale) + shift


def _adaln_kernel(c_ref, w_ref, b_ref, o_ref):
    s = _silu(c_ref[...])
    o_ref[...] = jnp.dot(s, w_ref[...], precision=HIGHEST, preferred_element_type=F32) + b_ref[...]


def _adaln(cvec, w_mod, b_mod, tc=512):
    depth, d, dm = w_mod.shape
    r = cvec.shape[0]
    return pl.pallas_call(
        _adaln_kernel,
        grid=(depth, dm // tc),
        in_specs=[pl.BlockSpec((r, d), lambda l, j: (0, 0)),
                  pl.BlockSpec((None, d, tc), lambda l, j: (l, 0, j)),
                  pl.BlockSpec((None, 1, tc), lambda l, j: (l, 0, j))],
        out_specs=pl.BlockSpec((None, r, tc), lambda l, j: (l, 0, j)),
        out_shape=jax.ShapeDtypeStruct((depth, r, dm), F32),
        compiler_params=_params(2),
        name="adaln",
    )(cvec, w_mod, b_mod.reshape(depth, 1, dm))


def _norm_matmul_kernel(x_ref, g_ref, sh_ref, sc_ref, w_ref, *rest, has_bias, glu, n_cast):
    n_extra = 1 if has_bias else 0
    o_ref = rest[n_extra + n_cast]
    _ride_along_cast(rest[n_extra:n_extra + n_cast], rest[n_extra + n_cast + 1:])
    h = _norm_mod(x_ref[...], g_ref[...], sh_ref[...], sc_ref[...])
    y = jnp.dot(h.astype(BF16), w_ref[...], preferred_element_type=F32)
    if has_bias:
        y = y + rest[0][...]
    if glu:
        half = y.shape[-1] // 2
        y = y[:, :half] * jax.nn.sigmoid(y[:, half:])
    o_ref[...] = y.astype(o_ref.dtype)


def _norm_matmul(x, g, mods, mod_row, shift_j, scale_j, w_bf16, bias=None, glu=False, tn=1024,
                 out_dtype=F32, cast_ws=(), cast_part=(0, 1)):
    b, n, d = x.shape
    nout = w_bf16.shape[1]
    tn = min(tn, n)
    out_w = nout // 2 if glu else nout
    in_specs = [pl.BlockSpec((None, tn, d), lambda i, j: (i, j, 0)),
                pl.BlockSpec((1, d), lambda i, j: (0, 0)),
                pl.BlockSpec((None, 1, d), lambda i, j: (mod_row(i) * N_MOD + shift_j, 0, 0)),
                pl.BlockSpec((None, 1, d), lambda i, j: (mod_row(i) * N_MOD + scale_j, 0, 0)),
                _single((d, nout), lambda i, j: (0, 0))]
    args = [x, g.reshape(1, d), mods, mods, w_bf16]
    if bias is not None:
        in_specs.append(pl.BlockSpec((1, nout), lambda i, j: (0, 0)))
        args.append(bias.reshape(1, nout))
    nt = n // tn
    slab_in, slab_out, slab_shapes = _slab_specs(cast_ws, *cast_part, b * nt, lambda i, j: i * nt + j)
    outs = pl.pallas_call(
        functools.partial(_norm_matmul_kernel, has_bias=bias is not None, glu=glu, n_cast=len(cast_ws)),
        grid=(b, nt),
        in_specs=in_specs + slab_in,
        out_specs=[pl.BlockSpec((None, tn, out_w), lambda i, j: (i, j, 0))] + slab_out,
        out_shape=[jax.ShapeDtypeStruct((b, n, out_w), out_dtype)] + slab_shapes,
        compiler_params=_params(2),
        name="norm_matmul",
    )(*args, *cast_ws)
    return outs if cast_ws else outs[0]


HALO = 16


def _shift_rows(cur, k, prev, nxt, has_prev, has_next):
    t = cur.shape[0]
    row = lax.broadcasted_iota(jnp.int32, cur.shape, 0)
    if k < 0:
        out = pltpu.roll(cur, -k, 0)
        for r in range(-k):
            src = prev[HALO + k + r:HALO + k + r + 1, :]
            fill = jnp.where(has_prev, src, 0.0)
            out = jnp.where(row == r, fill, out)
    else:
        out = pltpu.roll(cur, t - k, 0)
        for r in range(k):
            src = nxt[r:r + 1, :]
            fill = jnp.where(has_next, src, 0.0)
            out = jnp.where(row == t - k + r, fill, out)
    return out


def _dwconv_chunk(x_ref, s, t, n, w_ref, b_ref, pad_left):
    cur = x_ref[pl.ds(s, t), :].astype(F32)
    prev = x_ref[pl.ds(pl.multiple_of(jnp.maximum(s - HALO, 0), HALO), HALO), :].astype(F32)
    nxt = x_ref[pl.ds(pl.multiple_of(jnp.minimum(s + t, n - HALO), HALO), HALO), :].astype(F32)
    has_prev = s > 0
    has_next = s + t < n
    k_taps = w_ref.shape[0]
    acc = None
    for k in range(k_taps):
        off = k - pad_left
        xs = cur if off == 0 else _shift_rows(cur, off, prev, nxt, has_prev, has_next)
        term = xs * w_ref[k:k + 1, :]
        acc = term if acc is None else acc + term
    return acc + b_ref[...]


GATE_CHUNK = 256
SCAN_CHUNK = 64


def _scan_rows(a, b, carry, reverse):
    t = a.shape[0]
    row = lax.broadcasted_iota(jnp.int32, a.shape, 0)
    s = 1
    while s < t:
        if reverse:
            valid = row < t - s
            sh = t - s
        else:
            valid = row >= s
            sh = s
        a_s = jnp.where(valid, pltpu.roll(a, sh, 0), 1.0)
        b_s = jnp.where(valid, pltpu.roll(b, sh, 0), 0.0)
        b = a * b_s + b
        a = a * a_s
        s *= 2
    h = b + a * carry
    return h, (h[0:1] if reverse else h[t - 1:t])


def _gelu_tanh(x):
    return 0.5 * x * (1.0 + jnp.tanh(math.sqrt(2.0 / math.pi) * (x + 0.044715 * (x * x * x))))


def _rglru_kernel(*refs, n, with_gate):
    if with_gate:
        (xa_ref, gate_ref, cw_ref, cb_ref, wg_ref, bg_ref, lam_ref, h0_ref,
         y_ref, hl_ref, af_s, uf_s, ab_s, ub_s) = refs
    else:
        (xa_ref, cw_ref, cb_ref, wg_ref, bg_ref, lam_ref, h0_ref,
         hl_ref, af_s, uf_s, ab_s, ub_s) = refs
    w = xa_ref.shape[-1]
    tg = min(GATE_CHUNK, n)
    ts = min(SCAN_CHUNK, n)
    lam = lam_ref[...]
    sp = jnp.maximum(-lam, 0.0) + jnp.log(1.0 + jnp.exp(-jnp.abs(lam)))
    half_c_sp = (-0.5 * LRU_C) * sp

    def gates(ci, carry):
        s = pl.multiple_of(ci * tg, tg)
        xa = _dwconv_chunk(xa_ref, s, tg, n, cw_ref, cb_ref, 1)
        z = jnp.dot(xa.astype(BF16), wg_ref[...], preferred_element_type=F32) + bg_ref[...]
        xh = 0.5 * xa
        for d, (a_s, u_s) in enumerate(((af_s, uf_s), (ab_s, ub_s))):
            tr = jnp.tanh(0.5 * z[:, (2 * d) * w:(2 * d + 1) * w])
            ti = jnp.tanh(0.5 * z[:, (2 * d + 1) * w:(2 * d + 2) * w])
            c1 = half_c_sp[d:d + 1, :]
            a = jnp.exp(c1 * tr + c1)
            t = 1.0 - a * a
            root = jnp.where(t > 0.0, t * lax.rsqrt(t), 0.0)
            a_s[pl.ds(s, tg), :] = a
            u_s[pl.ds(s, tg), :] = root * (ti * xh + xh)
        return carry

    lax.fori_loop(0, n // tg, gates, 0)

    nc = n // ts

    def fwd(ci, carry):
        s = pl.multiple_of(ci * ts, ts)
        h, carry = _scan_rows(af_s[pl.ds(s, ts), :], uf_s[pl.ds(s, ts), :], carry, False)
        uf_s[pl.ds(s, ts), :] = h
        return carry

    hf_last = lax.fori_loop(0, nc, fwd, h0_ref[0:1, :])

    def bwd(ci, carry):
        s = pl.multiple_of((nc - 1 - ci) * ts, ts)
        h, carry = _scan_rows(ab_s[pl.ds(s, ts), :], ub_s[pl.ds(s, ts), :], carry, True)
        if with_gate:
            hsum = uf_s[pl.ds(s, ts), :] + h
            gate = gate_ref[pl.ds(s, ts), :].astype(F32)
            y_ref[pl.ds(s, ts), :] = (_gelu_tanh(gate) * hsum).astype(y_ref.dtype)
        return carry

    hb_last = lax.fori_loop(0, nc, bwd, h0_ref[1:2, :])
    hl_ref[0:1, :] = hf_last
    hl_ref[1:2, :] = hb_last


def _rglru(xa_src, xa_col, gate_src, gate_col, conv_w, conv_b, wg_bf16, bg, lam, h0, out_dtype=F32):
    b, n, _ = xa_src.shape
    w = conv_w.shape[1]
    with_gate = gate_src is not None
    seq = lambda col: pl.BlockSpec((None, n, w), lambda i: (i, 0, col))
    full = lambda shape: pl.BlockSpec(shape, lambda i: tuple(0 for _ in shape))
    in_specs = [seq(xa_col)]
    args = [xa_src]
    if with_gate:
        in_specs.append(seq(gate_col))
        args.append(gate_src)
    in_specs += [full(conv_w.shape), full((1, w)), _single(wg_bf16.shape, lambda i: (0, 0)),
                 full((1, 4 * w)), full((2, w)), pl.BlockSpec((None, 2, w), lambda i: (i, 0, 0))]
    args += [conv_w, conv_b.reshape(1, w), wg_bf16, bg.reshape(1, 4 * w), lam, h0]
    out_specs = [pl.BlockSpec((None, 2, w), lambda i: (i, 0, 0))]
    out_shape = [jax.ShapeDtypeStruct((b, 2, w), F32)]
    if with_gate:
        out_specs.insert(0, pl.BlockSpec((None, n, w), lambda i: (i, 0, 0)))
        out_shape.insert(0, jax.ShapeDtypeStruct((b, n, w), out_dtype))
    outs = pl.pallas_call(
        functools.partial(_rglru_kernel, n=n, with_gate=with_gate),
        grid=(b,),
        in_specs=in_specs,
        out_specs=out_specs,
        out_shape=out_shape,
        scratch_shapes=[pltpu.VMEM((n, w), F32)] * 4,
        compiler_params=_params(1),
        name="rglru_gated" if with_gate else "rglru_state",
    )(*args)
    return (outs[0], outs[1]) if with_gate else (None, outs[0])


def _lru_gate_weights(lru_wr, lru_wi, lru_br, lru_bi):
    def dense(wh):
        heads, hd, _ = wh.shape
        eye = jnp.eye(heads, dtype=wh.dtype)
        return jnp.einsum('hij,hg->higj', wh, eye).reshape(heads * hd, heads * hd)
    wg = jnp.concatenate([dense(lru_wr[0]), dense(lru_wi[0]), dense(lru_wr[1]), dense(lru_wi[1])], axis=1)
    bg = jnp.concatenate([lru_br[0], lru_bi[0], lru_br[1], lru_bi[1]], axis=0)
    return wg.astype(BF16), bg


@functools.lru_cache(maxsize=None)
def _dft_tables(n):
    m = 2 * n
    k = (np.arange(n, dtype=np.int64)[:, None] * np.arange(n, dtype=np.int64)[None, :]) % m
    ang = 2.0 * np.pi * np.arange(m, dtype=np.float64) / m
    cm = np.cos(ang)[k].astype(np.float32)
    sm = np.sin(ang)[k].astype(np.float32)
    return cm, sm


@functools.lru_cache(maxsize=None)
def _filter_features(n, emb):
    t = np.linspace(0.0, 1.0, n, dtype=np.float32)[:, None]
    bands = (emb - 1) // 2
    w = (np.float32(2.0 * math.pi / n) * np.arange(n, dtype=np.float32))[:, None]
    f = np.linspace(1e-4, bands - 1, bands, dtype=np.float32)[None, :]
    z = np.concatenate([t, np.cos(f * w), -np.sin(f * w)], axis=-1).astype(np.float32)
    return z


@functools.lru_cache(maxsize=None)
def _hyena_deltas(width):
    d = np.abs(np.linspace(math.log(HYENA_DECAY_TARGET) / HYENA_SLOW_PCT,
                           math.log(HYENA_DECAY_TARGET) / HYENA_FAST_PCT, width, dtype=np.float32))
    return d.reshape(1, width).astype(np.float32)


def _filter_kernel(z_ref, t_ref, w1_ref, b1_ref, w2_ref, b2_ref, w3_ref, b3_ref, fr_ref, dl_ref,
                   fwd_ref, bwd_ref):
    hw = dl_ref.shape[-1]
    hid = jnp.sin(fr_ref[0:1, :] * (jnp.dot(z_ref[...], w1_ref[...], precision=HIGHEST,
                                            preferred_element_type=F32) + b1_ref[...]))
    hid = jnp.sin(fr_ref[1:2, :] * (jnp.dot(hid, w2_ref[...], precision=HIGHEST,
                                            preferred_element_type=F32) + b2_ref[...]))
    filt = jnp.dot(hid, w3_ref[...], precision=HIGHEST, preferred_element_type=F32) + b3_ref[...]
    window = jnp.exp(-t_ref[...] * dl_ref[...]) + HYENA_SHIFT
    for o in range(HYENA_ORDER):
        fwd_ref[o] = filt[:, (2 * o) * hw:(2 * o + 1) * hw] * window
        bwd_ref[o] = filt[:, (2 * o + 1) * hw:(2 * o + 2) * hw] * window


def _hyena_filters(n, filt_w1, filt_b1, filt_w2, filt_b2, filt_w3, filt_b3, filt_freq, tn=256):
    emb, hidden = filt_w1.shape
    hw = filt_w3.shape[1] // (2 * HYENA_ORDER)
    tn = min(tn, n)
    z = jnp.asarray(_filter_features(n, emb))
    t = jnp.asarray(np.linspace(0.0, 1.0, n, dtype=np.float32)[:, None])
    dl = jnp.asarray(_hyena_deltas(hw))
    full = lambda shape: pl.BlockSpec(shape, lambda j: tuple(0 for _ in shape))
    return pl.pallas_call(
        _filter_kernel,
        grid=(n // tn,),
        in_specs=[pl.BlockSpec((tn, emb), lambda j: (j, 0)),
                  pl.BlockSpec((tn, 1), lambda j: (j, 0)),
                  full((emb, hidden)), full((1, hidden)), full((hidden, hidden)), full((1, hidden)),
                  full((hidden, 2 * HYENA_ORDER * hw)), full((1, 2 * HYENA_ORDER * hw)),
                  full((2, hidden)), full((1, hw))],
        out_specs=[pl.BlockSpec((HYENA_ORDER, tn, hw), lambda j: (0, j, 0))] * 2,
        out_shape=[jax.ShapeDtypeStruct((HYENA_ORDER, n, hw), F32)] * 2,
        compiler_params=_params(1),
        name="hyena_filter",
    )(z, t, filt_w1, filt_b1.reshape(1, -1), filt_w2, filt_b2.reshape(1, -1),
      filt_w3, filt_b3.reshape(1, -1), filt_freq, dl)


def _dot_hi_lo(t_hi, t_lo, f_cat):
    hw = f_cat.shape[-1] // 2
    p = jnp.dot(t_hi, f_cat, preferred_element_type=F32)
    return p[:, :hw] + p[:, hw:] + jnp.dot(t_lo, f_cat[:, :hw], preferred_element_type=F32)


HYENA_PHASES = 4


def _polyphase_filters(fwd, bwd):
    r = HYENA_PHASES
    order, n, hw = fwd.shape
    fp = fwd.reshape(order, n // r, r, hw)
    bp = bwd.reshape(order, n // r, r, hw)

    def shift1(a, first):
        return jnp.concatenate([first[:, None], a[:, :-1]], axis=1)

    def no_lag0(a):
        return a.at[:, 0].set(0.0)

    zero = jnp.zeros((order, hw), fwd.dtype)
    pos, neg = [], []
    for d in range(-(r - 1), r):
        if d == 0:
            pos.append(fp[:, :, 0])
            neg.append(no_lag0(bp[:, :, 0]))
        elif d > 0:
            pos.append(fp[:, :, d])
            neg.append(shift1(bp[:, :, r - d], zero))
        else:
            pos.append(shift1(fp[:, :, r + d], bp[:, 0, -d]))
            neg.append(no_lag0(bp[:, :, -d]))
    return jnp.stack(pos, axis=1), jnp.stack(neg, axis=1)


def _spectrum_kernel(ch_ref, cl_ref, sh_ref, sl_ref, kp_ref, km_ref, kr_ref, ki_ref, kn_ref):
    m = kp_ref.shape[0]
    ksum = kp_ref[...] + km_ref[...]
    kdif = km_ref[...] - kp_ref[...]
    row = lax.broadcasted_iota(jnp.int32, ksum.shape, 0)
    scale = jnp.where(row == 0, 1.0 / (2 * m), 2.0 / (2 * m))
    kr_ref[...] = scale * _dot_hi_lo(ch_ref[...], cl_ref[...], _split_hi_lo(ksum))
    ki_ref[...] = scale * _dot_hi_lo(sh_ref[...], sl_ref[...], _split_hi_lo(kdif))
    alt = jnp.where((row & 1) == 0, 1.0, -1.0)
    kn_ref[...] = jnp.sum(ksum * alt, axis=0, keepdims=True) * (1.0 / (2 * m))


def _hyena_spectrum(c_hi, c_lo, s_hi, s_lo, kpos, kneg):
    order, nd, m, hw = kpos.shape
    tab = pl.BlockSpec((m, m), lambda o, d: (0, 0))
    filt = pl.BlockSpec((None, None, m, hw), lambda o, d: (o, d, 0, 0))
    row = pl.BlockSpec((None, None, 1, hw), lambda o, d: (o, d, 0, 0))
    return pl.pallas_call(
        _spectrum_kernel,
        grid=(order, nd),
        in_specs=[tab, tab, tab, tab, filt, filt],
        out_specs=[filt, filt, row],
        out_shape=[jax.ShapeDtypeStruct((order, nd, m, hw), F32)] * 2
        + [jax.ShapeDtypeStruct((order, nd, 1, hw), F32)],
        compiler_params=_params(2),
        name="hyena_spectrum",
    )(c_hi, c_lo, s_hi, s_lo, kpos, kneg)


HYENA_CHUNK = 512


def _ride_along_cast(src_refs, dst_refs):
    for src, dst in zip(src_refs, dst_refs):
        dst[...] = src[...].astype(dst.dtype)


def _slab_specs(ws, part, parts, steps, step_of):
    ins, outs, shapes = [], [], []
    for w in ws:
        rows, cols = w.shape
        slab = rows // (parts * steps)
        ins.append(pl.BlockSpec((slab, cols), lambda *ids: (part * steps + step_of(*ids), 0)))
        outs.append(pl.BlockSpec((slab, cols), lambda *ids: (step_of(*ids), 0)))
        shapes.append(jax.ShapeDtypeStruct((rows // parts, cols), BF16))
    return ins, outs, shapes


def _hyena_kernel(v_ref, g0_ref, g1_ref, cwv_ref, cw0_ref, cw1_ref, cbv_ref, cb0_ref, cb1_ref,
                  cm_ref, sm_ref, kr_ref, ki_ref, kn_ref, fb_ref, *rest, n, n_cast):
    cast_src, z_ref, cast_dst = rest[:n_cast], rest[n_cast], rest[n_cast + 1:2 * n_cast + 1]
    u_s, g_s, x_s = rest[2 * n_cast + 1:]
    _ride_along_cast(cast_src, cast_dst)
    r = HYENA_PHASES
    ct = v_ref.shape[-1]
    m = n // r
    tc = min(HYENA_CHUNK, n)
    row = lax.broadcasted_iota(jnp.int32, (m, ct), 0)
    alt = jnp.where((row & 1) == 0, 1.0, -1.0)
    rc = min(128, m)

    groups = ct // LANES

    def put_rows(dst_s, rows, val):
        for j in range(groups):
            dst_s[j, rows, :] = val[:, j * LANES:(j + 1) * LANES]

    def get_rows(src_s, rows):
        return jnp.concatenate([src_s[j, rows, :] for j in range(groups)], axis=1)

    def short_conv(src_ref, cw_ref, cb_ref, dst_s):
        def body(ci, carry):
            s = pl.multiple_of(ci * tc, tc)
            put_rows(dst_s, pl.ds(s, tc), _dwconv_chunk(src_ref, s, tc, n, cw_ref, cb_ref, 1))
            return carry
        lax.fori_loop(0, n // tc, body, 0)

    def phase(ref, p):
        return get_rows(ref, pl.ds(p, m, stride=r))

    short_conv(v_ref, cwv_ref, cbv_ref, u_s)
    for o, (g_ref, cw_ref, cb_ref) in enumerate(((g0_ref, cw0_ref, cb0_ref), (g1_ref, cw1_ref, cb1_ref))):
        short_conv(g_ref, cw_ref, cb_ref, g_s)
        nyq = []
        for q in range(r):
            uq = phase(u_s, q)
            ub = uq.astype(BF16)
            x_s[q, 0] = jnp.dot(cm_ref[...], ub, preferred_element_type=F32)
            x_s[q, 1] = jnp.dot(sm_ref[...], ub, preferred_element_type=F32)
            nyq.append(jnp.sum(uq * alt, axis=0, keepdims=True))
        for p in range(r):
            zc_parts, zs_parts = [], []
            for r0 in range(0, m, rc):
                zc = zs = None
                for q in range(r):
                    d = p - q + r - 1
                    xre = x_s[q, 0, r0:r0 + rc, :]
                    xim = x_s[q, 1, r0:r0 + rc, :]
                    kr = kr_ref[o, d, r0:r0 + rc, :]
                    ki = ki_ref[o, d, r0:r0 + rc, :]
                    tc_ = xre * kr + xim * ki
                    ts_ = xim * kr - xre * ki
                    zc = tc_ if zc is None else zc + tc_
                    zs = ts_ if zs is None else zs + ts_
                zc_parts.append(zc.astype(BF16))
                zs_parts.append(zs.astype(BF16))
            zc = jnp.concatenate(zc_parts, axis=0)
            zs = jnp.concatenate(zs_parts, axis=0)
            y = (jnp.dot(cm_ref[...], zc, preferred_element_type=F32)
                 + jnp.dot(sm_ref[...], zs, preferred_element_type=F32))
            ynyq = None
            for q in range(r):
                t = nyq[q] * kn_ref[o, p - q + r - 1]
                ynyq = t if ynyq is None else ynyq + t
            y = y + alt * ynyq + phase(u_s, p) * fb_ref[o]
            put_rows(u_s, pl.ds(p, m, stride=r), phase(g_s, p) * y)

    def emit(ci, carry):
        rows = pl.ds(pl.multiple_of(ci * tc, tc), tc)
        z_ref[rows, :] = get_rows(u_s, rows).astype(z_ref.dtype)
        return carry

    lax.fori_loop(0, n // tc, emit, 0)


def _hyena(proj, hy_col0, conv_w, conv_b, cm_bf16, sm_bf16, kr, ki, kn, filt_bias, ct=256,
           out_dtype=F32, cast_ws=(), cast_part=(0, 1)):
    b, n, _ = proj.shape
    hw = kr.shape[-1]
    nd, m = kr.shape[1], kr.shape[2]
    ct = min(ct, hw)
    nct = hw // ct
    base = hy_col0 // ct
    k_taps = conv_w.shape[0]

    def seq(j):
        return pl.BlockSpec((None, n, ct), lambda c, i: (i, 0, base + j * nct + c))

    def cw(j):
        return pl.BlockSpec((k_taps, ct), lambda c, i: (0, j * nct + c))

    def cb(j):
        return pl.BlockSpec((1, ct), lambda c, i: (0, j * nct + c))

    kspec = _single((HYENA_ORDER, nd, m, ct), lambda c, i: (0, 0, 0, c))
    knspec = pl.BlockSpec((HYENA_ORDER, nd, 1, ct), lambda c, i: (0, 0, 0, c))
    rowspec = pl.BlockSpec((HYENA_ORDER, 1, ct), lambda c, i: (0, 0, c))
    conv_b2 = conv_b.reshape(1, -1)
    slab_in, slab_out, slab_shapes = _slab_specs(cast_ws, *cast_part, nct * b, lambda c, i: c * b + i)
    return pl.pallas_call(
        functools.partial(_hyena_kernel, n=n, n_cast=len(cast_ws)),
        grid=(nct, b),
        in_specs=[seq(0), seq(1), seq(2), cw(0), cw(1), cw(2), cb(0), cb(1), cb(2),
                  pl.BlockSpec((m, m), lambda c, i: (0, 0)), pl.BlockSpec((m, m), lambda c, i: (0, 0)),
                  kspec, kspec, knspec, rowspec] + slab_in,
        out_specs=[pl.BlockSpec((None, n, ct), lambda c, i: (i, 0, c))] + slab_out,
        out_shape=[jax.ShapeDtypeStruct((b, n, hw), out_dtype)] + slab_shapes,
        scratch_shapes=[pltpu.VMEM((ct // LANES, n, LANES), F32), pltpu.VMEM((ct // LANES, n, LANES), F32),
                        pltpu.VMEM((HYENA_PHASES, 2, m, ct), F32)],
        compiler_params=_params(2),
        name="hyena",
    )(proj, proj, proj, conv_w, conv_w, conv_w, conv_b2, conv_b2, conv_b2,
      cm_bf16, sm_bf16, kr, ki, kn, filt_bias.reshape(HYENA_ORDER, 1, hw), *cast_ws)


def _outproj_kernel(*refs, n_act, has_bias):
    acts = refs[:n_act]
    (w_ref, x_ref, g1_ref, ng_ref, sh_ref, sc_ref, rt_ref) = refs[n_act:n_act + 7]
    rest = refs[n_act + 7:]
    x1_ref, h2_ref, lg_ref = rest[-3:]
    y = None
    k0 = 0
    for a_ref in acts:
        k = a_ref.shape[-1]
        part = jnp.dot(a_ref[...].astype(BF16), w_ref[k0:k0 + k, :], preferred_element_type=F32)
        y = part if y is None else y + part
        k0 += k
    if has_bias:
        y = y + rest[0][...]
    x1 = x_ref[...] + g1_ref[...] * y
    x1_ref[...] = x1
    h2 = _norm_mod(x1, ng_ref[...], sh_ref[...], sc_ref[...])
    h_hi = h2.astype(BF16)
    h_lo = (h2 - h_hi.astype(F32)).astype(BF16)
    half = h2.shape[-1] // 2
    h2_ref[...] = pltpu.pack_elementwise([h2[:, :half], h2[:, half:]], packed_dtype=BF16)
    ne = lg_ref.shape[0]
    nt = (((1,), (1,)), ((), ()))
    p_hi = lax.dot_general(rt_ref[...], h_hi, nt, preferred_element_type=F32)
    p_lo = lax.dot_general(rt_ref[0:ne, :], h_lo, nt, preferred_element_type=F32)
    lg_ref[...] = p_hi[0:ne, :] + p_hi[ne:2 * ne, :] + p_lo


def _outproj(acts, w_bf16, bias, x, mods, norm_g, router, tn=1024):
    b, n, d = x.shape
    e = router.shape[1]
    tn = min(tn, n)
    row = lambda j: pl.BlockSpec((None, 1, d), lambda i, t: (i * N_MOD + j, 0, 0))
    in_specs = [pl.BlockSpec((None, tn, a.shape[-1]), lambda i, t: (i, t, 0)) for a in acts]
    in_specs += [_single(w_bf16.shape, lambda i, t: (0, 0)),
                 pl.BlockSpec((None, tn, d), lambda i, t: (i, t, 0)),
                 row(2), pl.BlockSpec((1, d), lambda i, t: (0, 0)), row(3), row(4),
                 pl.BlockSpec((2 * e, d), lambda i, t: (0, 0))]
    rt = router.T
    rt_hi = rt.astype(BF16)
    rt_lo = (rt - rt_hi.astype(F32)).astype(BF16)
    args = list(acts) + [w_bf16, x, mods, norm_g.reshape(1, d), mods, mods,
                         jnp.concatenate([rt_hi, rt_lo], axis=0)]
    if bias is not None:
        in_specs.append(pl.BlockSpec((1, d), lambda i, t: (0, 0)))
        args.append(bias.reshape(1, d))
    return pl.pallas_call(
        functools.partial(_outproj_kernel, n_act=len(acts), has_bias=bias is not None),
        grid=(b, n // tn),
        in_specs=in_specs,
        out_specs=[pl.BlockSpec((None, tn, d), lambda i, t: (i, t, 0)),
                   pl.BlockSpec((None, tn, d // 2), lambda i, t: (i, t, 0)),
                   pl.BlockSpec((None, e, tn), lambda i, t: (i, 0, t))],
        out_shape=[jax.ShapeDtypeStruct((b, n, d), F32),
                   jax.ShapeDtypeStruct((b, n, d // 2), jnp.uint32),
                   jax.ShapeDtypeStruct((b, e, n), F32)],
        compiler_params=_params(2),
        name="outproj_router",
    )(*args)


def _cumsum_lanes(x):
    n = x.shape[-1]
    lane = lax.broadcasted_iota(jnp.int32, x.shape, 1)
    s = 1
    while s < n:
        x = x + jnp.where(lane >= s, pltpu.roll(x, s, 1), 0)
        s *= 2
    return x


def _compact_lanes(slot, sel, payloads):
    n = slot.shape[-1]
    lane = lax.broadcasted_iota(jnp.int32, slot.shape, 1)
    enc = jnp.where(sel, ((lane - slot) << 1) | 1, 0)
    j = 0
    while (1 << j) < n:
        sh = 1 << j
        inside = lane < n - sh
        src_enc = jnp.where(inside, pltpu.roll(enc, n - sh, 1), 0)
        take = ((src_enc >> (j + 1)) & 1) > 0
        stay = (enc & (1 - ((enc >> (j + 1)) & 1))) > 0
        payloads = [jnp.where(take, pltpu.roll(v, n - sh, 1), v) for v in payloads]
        enc = jnp.where(take, src_enc, jnp.where(stay, enc, 0))
        j += 1
    return payloads


def _route_kernel(lg_ref, idx_ref, gslot_ref, *, cap):
    for s in range(lg_ref.shape[0]):
        _route_one(lg_ref.at[s], idx_ref.at[s], gslot_ref.at[s], cap)


def _route_one(lg_ref, idx_ref, gslot_ref, cap):
    lg = lg_ref[...]
    m = jnp.max(lg, axis=0, keepdims=True)
    ex = jnp.exp(lg - m)
    probs = ex / jnp.sum(ex, axis=0, keepdims=True)
    thr = jnp.zeros((lg.shape[0], 1), jnp.int32)
    for bit in range(30, -1, -1):
        cand = thr | (1 << bit)
        cnt = jnp.sum(jnp.where(probs >= pltpu.bitcast(cand, F32), 1.0, 0.0), axis=1, keepdims=True)
        thr = jnp.where(cnt >= cap, cand, thr)
    gt = probs >= pltpu.bitcast(thr + 1, F32)
    eq = jnp.logical_and(probs >= pltpu.bitcast(thr, F32), jnp.logical_not(gt))
    need = cap - jnp.sum(jnp.where(gt, 1.0, 0.0), axis=1, keepdims=True)
    eq_rank = _cumsum_lanes(jnp.where(eq, 1.0, 0.0))
    sel = jnp.where(gt, 1.0, jnp.where(eq, jnp.where(eq_rank <= need, 1.0, 0.0), 0.0))
    slot = (_cumsum_lanes(sel) - 1.0).astype(jnp.int32)
    lane = lax.broadcasted_iota(jnp.int32, slot.shape, 1)
    idx, gslot = _compact_lanes(slot, sel > 0.0, [lane, probs])
    idx_ref[...] = idx[:, :idx_ref.shape[-1]]
    gslot_ref[...] = gslot[:, :gslot_ref.shape[-1]]


ROUTE_SAMPLES = 4


def _route(logits, cap):
    b, e, n = logits.shape
    g = math.gcd(ROUTE_SAMPLES, b)
    capw = -(-cap // LANES) * LANES
    slots = pl.BlockSpec((g, e, capw), lambda i: (i, 0, 0))
    return pl.pallas_call(
        functools.partial(_route_kernel, cap=cap),
        grid=(b // g,),
        in_specs=[pl.BlockSpec((g, e, n), lambda i: (i, 0, 0))],
        out_specs=[slots, slots],
        out_shape=[jax.ShapeDtypeStruct((b, e, capw), jnp.int32),
                   jax.ShapeDtypeStruct((b, e, capw), F32)],
        compiler_params=_params(1),
        name="route",
    )(logits)


MOE_ROW_CHUNK = 512
SCATTER_GROUP = 4


def _moe_kernel(idx_ref, gs_ref, h_ref, x_ref, g2_ref, w1_ref, w3_ref, w2_ref, *rest, cap, final,
                n_cast):
    n_extra = 1 if final else 0
    cast_src = rest[n_extra:n_extra + n_cast]
    o_ref = rest[n_extra + n_cast]
    cast_dst = rest[n_extra + n_cast + 1:n_extra + 2 * n_cast + 1]
    xs_s, ys_s = rest[n_extra + 2 * n_cast + 1:]
    _ride_along_cast(cast_src, cast_dst)
    e = pl.program_id(1)
    ne = pl.num_programs(1)
    n = h_ref.shape[0]
    tr = min(MOE_ROW_CHUNK, n)

    def gather_rows(k):
        for c in range(cap):
            xs_s[c:c + 1, :] = h_ref[pl.ds(idx_ref[k, c], 1), :]

    def scatter_rows(k, live):
        for c0 in range(0, cap, SCATTER_GROUP):
            rows = [pl.ds(idx_ref[k, c], 1) for c in range(c0, c0 + SCATTER_GROUP)]
            new = [o_ref[row, :] + (gs_ref[k, c0 + j] * live) * ys_s[c0 + j:c0 + j + 1, :]
                   for j, row in enumerate(rows)]
            for row, val in zip(rows, new):
                o_ref[row, :] = val

    @pl.when(e == 0)
    def _():
        o_ref[...] = x_ref[...]
        ys_s[...] = jnp.zeros_like(ys_s)
        gather_rows(0)

    words = xs_s[...]
    xs = jnp.concatenate(
        [pltpu.unpack_elementwise(words, index=i, packed_dtype=BF16, unpacked_dtype=F32) for i in (0, 1)],
        axis=1).astype(BF16)
    gather_rows(jnp.minimum(e + 1, ne - 1))
    scatter_rows(jnp.maximum(e - 1, 0), jnp.where(e > 0, 1.0, 0.0))
    a = jnp.dot(xs, w1_ref[...], preferred_element_type=F32)
    g = jnp.dot(xs, w3_ref[...], preferred_element_type=F32)
    y = jnp.dot((_silu(a) * g).astype(BF16), w2_ref[...], preferred_element_type=F32)
    ys_s[...] = y * g2_ref[...]

    @pl.when(e == ne - 1)
    def _():
        scatter_rows(ne - 1, 1.0)

    if final:
        @pl.when(e == ne - 1)
        def _():
            def norm_rows(ci, carry):
                rows = pl.ds(pl.multiple_of(ci * tr, tr), tr)
                xf = o_ref[rows, :]
                ms = jnp.mean(xf * xf, axis=-1, keepdims=True)
                o_ref[rows, :] = xf * lax.rsqrt(ms + EPS) * rest[0][...]
                return carry
            lax.fori_loop(0, n // tr, norm_rows, 0)


def _moe(h2, idx, gslot, x1, mods, w1, w3, w2, final_g=None, cast_ws=(), cast_part=(0, 1)):
    b, n, d = x1.shape
    e = idx.shape[1]
    f = w1.shape[-1]
    cap = EC_CAPACITY * n // e
    final = final_g is not None
    table = pl.BlockSpec((None, e, idx.shape[-1]), lambda i, k: (i, 0, 0), memory_space=pltpu.SMEM)
    in_specs = [table, table,
                pl.BlockSpec((None, n, d // 2), lambda i, k: (i, 0, 0)),
                pl.BlockSpec((None, n, d), lambda i, k: (i, 0, 0)),
                pl.BlockSpec((None, 1, d), lambda i, k: (i * N_MOD + 5, 0, 0)),
                pl.BlockSpec((None, d, f), lambda i, k: (k, 0, 0)),
                pl.BlockSpec((None, d, f), lambda i, k: (k, 0, 0)),
                pl.BlockSpec((None, f, d), lambda i, k: (k, 0, 0))]
    args = [idx, gslot, h2, x1, mods, w1, w3, w2]
    if final:
        in_specs.append(pl.BlockSpec((1, d), lambda i, k: (0, 0)))
        args.append(final_g.reshape(1, d))
    slab_in, slab_out, slab_shapes = _slab_specs(cast_ws, *cast_part, b * e, lambda i, k: i * e + k)
    outs = pl.pallas_call(
        functools.partial(_moe_kernel, cap=cap, final=final, n_cast=len(cast_ws)),
        grid=(b, e),
        in_specs=in_specs + slab_in,
        out_specs=[pl.BlockSpec((None, n, d), lambda i, k: (i, 0, 0))] + slab_out,
        out_shape=[jax.ShapeDtypeStruct((b, n, d), F32)] + slab_shapes,
        scratch_shapes=[pltpu.VMEM((cap, d // 2), jnp.uint32), pltpu.VMEM((cap, d), F32)],
        compiler_params=_params(2),
        name="moe",
    )(*args, *cast_ws)
    return outs if cast_ws else outs[0]


VCONV_GROUP = 4


def _vconv_kernel(u_ref, w_ref, b_ref, lg_ref, lb_ref, o_ref, pad_s, y_s, *, n, k_taps, group):
    d = u_ref.shape[-1]
    groups = d // LANES
    n_rows = n // GRID_W
    top = (k_taps - 1) // 2
    halo = (group - 1) * GRID_W
    span = group * GRID_W
    tcp = min(256, n)

    for j in range(groups):
        if halo:
            pad_s[j, 0:halo, :] = jnp.zeros((halo, LANES), F32)
            pad_s[j, halo + n:halo + n + halo, :] = jnp.zeros((halo, LANES), F32)

    def fill(ci, carry):
        s = pl.multiple_of(ci * tcp, tcp)
        x = u_ref[pl.ds(s, tcp), :].astype(F32)
        for j in range(groups):
            pad_s[j, pl.ds(halo + s, tcp), :] = x[:, j * LANES:(j + 1) * LANES]
        return carry

    lax.fori_loop(0, n // tcp, fill, 0)

    for g in range(n_rows // group):
        r0 = g * group
        taps = [k for k in range(k_taps) if any(0 <= r0 + j + k - top < n_rows for j in range(group))]

        def lane_group(cj, carry, r0=r0, taps=taps):
            acc = None
            for k in taps:
                s = (r0 + k - top) * GRID_W + halo
                term = pad_s[cj, s:s + span, :] * w_ref[cj, k:k + 1, :]
                acc = term if acc is None else acc + term
            y_s[cj] = acc + b_ref[cj]
            return carry

        lax.fori_loop(0, groups, lane_group, 0)
        for j in range(group):
            acc = jnp.concatenate([y_s[cj, j * GRID_W:(j + 1) * GRID_W, :] for cj in range(groups)], axis=1)
            mu = jnp.mean(acc, axis=-1, keepdims=True)
            xc = acc - mu
            var = jnp.mean(xc * xc, axis=-1, keepdims=True)
            y = xc * lax.rsqrt(var + EPS) * lg_ref[...] + lb_ref[...]
            o_ref[(r0 + j) * GRID_W:(r0 + j + 1) * GRID_W, :] = _silu(y).astype(o_ref.dtype)


def _vconv_ln_silu(u, dw_w, dw_b, ln_g, ln_b):
    b, n, d = u.shape
    k_taps = dw_w.shape[0]
    groups = d // LANES
    group = math.gcd(VCONV_GROUP, n // GRID_W)
    halo = (group - 1) * GRID_W
    full = lambda shape: pl.BlockSpec(shape, lambda i: tuple(0 for _ in shape))
    w3 = dw_w.reshape(k_taps, groups, LANES).transpose(1, 0, 2)
    b3 = dw_b.reshape(groups, 1, LANES)
    return pl.pallas_call(
        functools.partial(_vconv_kernel, n=n, k_taps=k_taps, group=group),
        grid=(b,),
        in_specs=[pl.BlockSpec((None, n, d), lambda i: (i, 0, 0)),
                  full((groups, k_taps, LANES)), full((groups, 1, LANES)), full((1, d)), full((1, d))],
        out_specs=pl.BlockSpec((None, n, d), lambda i: (i, 0, 0)),
        out_shape=jax.ShapeDtypeStruct((b, n, d), BF16),
        scratch_shapes=[pltpu.VMEM((groups, n + 2 * halo, LANES), F32),
                        pltpu.VMEM((groups, group * GRID_W, LANES), F32)],
        compiler_params=_params(1),
        name="vconv_ln_silu",
    )(u, w3, b3, ln_g.reshape(1, d), ln_b.reshape(1, d))


def kernel(x, c, ctx, c_ctx, norm1_g, norm2_g, w_mod, b_mod, w_in, conv_a_w, conv_a_b, lru_wr, lru_br, lru_wi, lru_bi, lru_lam, conv_b_w, conv_b_b, filt_w1, filt_b1, filt_w2, filt_b2, filt_w3, filt_b3, filt_freq, filt_bias, w_out, cf_w1, cf_b1, cf_dw_w, cf_dw_b, cf_ln_g, cf_ln_b, cf_w2, cf_b2, router, exp_w1, exp_w3, exp_w2, final_g):
    bsz, n, d = x.shape
    depth = w_mod.shape[0]
    assert depth == 2 and w_in.shape[0] == 1 and cf_w1.shape[0] == 1
    rw = conv_a_w.shape[-1]
    hw = filt_bias.shape[-1]
    e = router.shape[-1]
    cap = EC_CAPACITY * n // e

    rows = -(-(bsz + 1) // SUBLANES) * SUBLANES
    cvec = jnp.concatenate([c, c_ctx[None], jnp.zeros((rows - bsz - 1, d), F32)], axis=0)
    mods_all = _adaln(cvec, w_mod, b_mod)
    mods = [mods_all[l].reshape(rows * N_MOD, 1, d) for l in range(depth)]

    w_in0 = w_in[0].astype(BF16)
    wg, bg = _lru_gate_weights(lru_wr[0], lru_wi[0], lru_br[0], lru_bi[0])
    ctx_xa = _norm_matmul(ctx, norm1_g[0], mods[0], lambda i: bsz, 0, 1, w_in0[:, rw:2 * rw],
                          out_dtype=BF16)
    h0_ctx = jnp.zeros((bsz, 2, rw), F32)
    _, ctx_state = _rglru(ctx_xa, 0, None, 0, conv_a_w[0], conv_a_b[0], wg, bg, lru_lam[0], h0_ctx)
    proj = _norm_matmul(x, norm1_g[0], mods[0], lambda i: i, 0, 1, w_in0, out_dtype=BF16)
    y_a, _ = _rglru(proj, 1, proj, 0, conv_a_w[0], conv_a_b[0], wg, bg, lru_lam[0], ctx_state,
                    out_dtype=BF16)

    fwd, bwd = _hyena_filters(n, filt_w1[0], filt_b1[0], filt_w2[0], filt_b2[0],
                              filt_w3[0], filt_b3[0], filt_freq[0])
    kpos, kneg = _polyphase_filters(fwd, bwd)
    cm, sm = (jnp.asarray(t) for t in _dft_tables(n // HYENA_PHASES))
    cm_hi, sm_hi = cm.astype(BF16), sm.astype(BF16)
    cm_lo = (cm - cm_hi.astype(F32)).astype(BF16)
    sm_lo = (sm - sm_hi.astype(F32)).astype(BF16)
    kr, ki, kn = _hyena_spectrum(cm_hi, cm_lo, sm_hi, sm_lo, kpos, kneg)
    ff = exp_w1.shape[-1]
    expert_ws = [exp_w1.reshape(-1, ff), exp_w3.reshape(-1, ff), exp_w2.reshape(-1, d)]
    expert_shapes = [(e, d, ff), (e, d, ff), (e, ff, d)]
    z, *ew = _hyena(proj, 2 * rw, conv_b_w[0], conv_b_b[0], cm_hi, sm_hi,
                    kr, ki, kn, filt_bias[0], out_dtype=BF16, cast_ws=expert_ws, cast_part=(0, depth))
    ew = [w.reshape(s) for w, s in zip(ew, expert_shapes)]

    x1, h2, logits = _outproj([y_a, z], w_out[0].astype(BF16), None, x, mods[0], norm2_g[0], router[0])
    idx, gslot = _route(logits, cap)
    xl = _moe(h2, idx, gslot, x1, mods[0], *ew)

    u, *ew_next = _norm_matmul(xl, norm1_g[1], mods[1], lambda i: i, 0, 1, cf_w1[0].astype(BF16),
                               bias=cf_b1[0], glu=True, out_dtype=BF16,
                               cast_ws=expert_ws, cast_part=(1, depth))
    ew_next = [w.reshape(s) for w, s in zip(ew_next, expert_shapes)]
    v = _vconv_ln_silu(u, cf_dw_w[0], cf_dw_b[0], cf_ln_g[0], cf_ln_b[0])
    x1, h2, logits = _outproj([v], cf_w2[0].astype(BF16), cf_b2[0], xl, mods[1], norm2_g[1], router[1])
    idx, gslot = _route(logits, cap)
    return _moe(h2, idx, gslot, x1, mods[1], *ew_next, final_g=final_g)
```
